```python
import math
import jax
import jax.numpy as jnp
from jax import lax
import numpy as np

D_MODEL = 4096
BATCH = 4
SEQ = 2048
DEPTH = 4
DEC_BATCH = 32
DEC_SEQ = 1
PAST_LEN = 8192
PAGE_SIZE = 128

HEAD_DIM = 128
N_HEADS = D_MODEL // HEAD_DIM
N_KV_HEADS = N_HEADS // 4
GQA = N_HEADS // N_KV_HEADS
Q_SCALE = HEAD_DIM ** -0.5
N_MIXERS = 3
SWA_WINDOW = 128
MOBA_BLOCK = 256
MOBA_TOPK = 3
MOBA_Q_CHUNK = 8
DIL_PAIRS = ((128, 1), (512, 4), (2048, 16))
N_DIL = len(DIL_PAIRS)
DIL_BAND = 128
DIL_KV_HEADS = N_KV_HEADS // 2
DIL_GQA = N_HEADS // DIL_KV_HEADS
REL_BUCKETS = 32
REL_MAX_DISTANCE = 2048
N_EXPERTS = 16
N_EXPERT_GROUPS = 4
EXPERTS_PER_GROUP = N_EXPERTS // N_EXPERT_GROUPS
TOP_GROUPS = 1
TOP_K = 2
GROUP_SCORE_TOPK = 2
D_EXPERT = D_MODEL // 4
DN_ALPHA = (2 * DEPTH) ** 0.25
DN_BETA = (8 * DEPTH) ** -0.25
LN_EPS = 1e-5
LAYER_KIND = tuple(i % N_MIXERS for i in range(DEPTH))
LAYER_SLOT = tuple(LAYER_KIND[:i].count(LAYER_KIND[i]) for i in range(DEPTH))
N_A = LAYER_KIND.count(0)
N_B = LAYER_KIND.count(1)
N_C = LAYER_KIND.count(2)

kernel_name = 'hybrid_swa_moba_dilated_moe_decoder_step'


def layer_norm(x, g, b):
    xf = x.astype(jnp.float32)
    mu = jnp.mean(xf, axis=-1, keepdims=True)
    var = jnp.mean(jnp.square(xf - mu), axis=-1, keepdims=True)
    y = (xf - mu) * lax.rsqrt(var + LN_EPS)
    return (y * g.astype(jnp.float32) + b.astype(jnp.float32)).astype(x.dtype)


def rel_bucket(dist):
    n = jnp.maximum(dist, 0)
    exact = REL_BUCKETS // 2
    scaled = (jnp.log(jnp.maximum(n, exact).astype(jnp.float32) / exact)
              / math.log(REL_MAX_DISTANCE / exact) * (REL_BUCKETS - exact))
    large = jnp.minimum(exact + scaled.astype(jnp.int32), REL_BUCKETS - 1)
    return jnp.where(n < exact, n, large)


def rel_bias(dist, rel_table):
    return rel_table.astype(jnp.float32)[rel_bucket(dist)]


def adaln(c, w, b):
    m = jax.nn.silu(c) @ w + b
    return jnp.split(m[:, None, :], 6, axis=-1)


def modulate(x, shift, scale):
    return x * (1.0 + scale) + shift


def project_qkv(h, w_qkv, kv_tail):
    b, s = h.shape[:2]
    qkv = h @ w_qkv
    q = qkv[..., :N_HEADS * HEAD_DIM].reshape(b, s, N_HEADS, HEAD_DIM) * Q_SCALE
    kv = qkv[..., N_HEADS * HEAD_DIM:].reshape((b, s) + kv_tail)
    return q, kv


def out_proj(o, w_o):
    b, s = o.shape[:2]
    return o.reshape(b, s, N_HEADS * HEAD_DIM) @ w_o


def sink_softmax(logits, sink):
    m = jnp.maximum(jnp.max(logits, axis=-1, keepdims=True), sink)
    p = jnp.exp(logits - m)
    return p / (jnp.sum(p, axis=-1, keepdims=True) + jnp.exp(sink - m))


def swa_prompt(q, kv, sinks, rel_table):
    b, s = q.shape[:2]
    w = SWA_WINDOW
    nb = s // w
    qb = q.reshape(b, nb, w, N_KV_HEADS, GQA, HEAD_DIM)
    kvb = kv.reshape(b, nb, w, 2, N_KV_HEADS, HEAD_DIM)
    prev = jnp.pad(kvb, ((0, 0), (1, 0), (0, 0), (0, 0), (0, 0), (0, 0)))[:, :-1]
    band = jnp.concatenate([prev, kvb], axis=2)
    logits = jnp.einsum('bnqkgd,bnskd->bnkgqs', qb, band[:, :, :, 0]).astype(jnp.float32)
    dist = (jnp.arange(w)[:, None] + w) - jnp.arange(2 * w)[None, :]
    key_pos = jnp.arange(nb)[:, None, None] * w - w + jnp.arange(2 * w)[None, None, :]
    keep = (dist >= 0) & (dist < w) & (key_pos >= 0)
    bias = jnp.transpose(rel_bias(dist, rel_table), (2, 0, 1)).reshape(N_KV_HEADS, GQA, w, 2 * w)
    logits = jnp.where(keep[None, :, None, None], logits + bias, -jnp.inf)
    p = sink_softmax(logits, sinks.astype(jnp.float32).reshape(N_KV_HEADS, GQA, 1, 1))
    out = jnp.einsum('bnkgqs,bnskd->bnqkgd', p.astype(kv.dtype), band[:, :, :, 1])
    return out.reshape(b, s, N_HEADS, HEAD_DIM)


def swa_sample(q, kv_new, buf, sinks, rel_table):
    b, t = q.shape[:2]
    lb = buf.shape[1]
    kk = jnp.concatenate([buf, kv_new], axis=1)
    dist = (lb + jnp.arange(t))[:, None] - jnp.arange(lb + t)[None, :]
    keep = (dist >= 0) & (dist < SWA_WINDOW)
    qg = q.reshape(b, t, N_KV_HEADS, GQA, HEAD_DIM)
    logits = jnp.einsum('btkgd,bskd->bkgts', qg, kk[:, :, 0]).astype(jnp.float32)
    bias = jnp.transpose(rel_bias(dist, rel_table), (2, 0, 1)).reshape(N_KV_HEADS, GQA, t, lb + t)
    logits = jnp.where(keep, logits + bias, -jnp.inf)
    p = sink_softmax(logits, sinks.astype(jnp.float32).reshape(N_KV_HEADS, GQA, 1, 1))
    out = jnp.einsum('bkgts,bskd->btkgd', p.astype(kk.dtype), kk[:, :, 1]).reshape(b, t, N_HEADS, HEAD_DIM)
    new_len = min(SWA_WINDOW, lb + t)
    return out, kk[:, lb + t - new_len:]


def mixer_swa(hp, hs, buf, w_qkv, w_o, sinks, rel_table):
    tail = (2, N_KV_HEADS, HEAD_DIM)
    qp, kvp = project_qkv(hp, w_qkv, tail)
    qs, kvs = project_qkv(hs, w_qkv, tail)
    s = hp.shape[1]
    op = swa_prompt(qp, kvp, sinks, rel_table)
    os_, new_buf = swa_sample(qs, kvs, buf, sinks, rel_table)
    return out_proj(op, w_o), out_proj(os_, w_o), kvp[:, s - min(SWA_WINDOW, s):], new_buf


def moba_attention(q, kv, q_pos, rel_table, q_chunk):
    b, sq = q.shape[:2]
    L = kv.shape[1]
    nb = -(-L // MOBA_BLOCK)
    kvb = jnp.pad(kv, ((0, 0), (0, nb * MOBA_BLOCK - L), (0, 0), (0, 0), (0, 0)))
    kvb = kvb.reshape(b, nb, MOBA_BLOCK, 2, N_KV_HEADS, HEAD_DIM)
    kmean = jnp.mean(kvb[:, :, :, 0], axis=2, dtype=jnp.float32)
    n_sel = min(MOBA_TOPK, nb)
    tab = rel_table.astype(jnp.float32).T.reshape(N_KV_HEADS, GQA, REL_BUCKETS)
    b_i = jnp.arange(b)[:, None, None, None]
    kv_i = jnp.arange(N_KV_HEADS)[None, None, :, None]
    t_i = jnp.arange(MOBA_BLOCK)

    def chunk(args):
        qc, pc = args
        qc = qc.reshape(b, q_chunk, N_KV_HEADS, GQA, HEAD_DIM)
        own = pc[0] // MOBA_BLOCK
        gate = jnp.einsum('bqkgd,bnkd->bqkn', qc.astype(jnp.float32), kmean)
        gate = jnp.where(jnp.arange(nb) < own, gate, -jnp.inf)
        _, blk = lax.top_k(gate, n_sel)
        sel = kvb[b_i, blk, :, :, kv_i]
        s_log = jnp.einsum('bqkgd,bqkntd->bqkgnt', qc, sel[..., 0, :]).astype(jnp.float32)
        s_dist = pc[None, :, None, None, None] - (blk[..., None] * MOBA_BLOCK + t_i)
        s_bias = tab[jnp.arange(N_KV_HEADS)[:, None, None, None], jnp.arange(GQA)[:, None, None],
                     rel_bucket(s_dist)[:, :, :, None]]
        s_keep = (blk < own)[:, :, :, None, :, None]
        s_log = jnp.where(s_keep, s_log + s_bias, -jnp.inf)
        own_kv = lax.dynamic_index_in_dim(kvb, own, axis=1, keepdims=False)
        o_log = jnp.einsum('bqkgd,btkd->bqkgt', qc, own_kv[:, :, 0]).astype(jnp.float32)
        o_dist = pc[:, None] - (own * MOBA_BLOCK + t_i)[None, :]
        o_bias = jnp.moveaxis(rel_bias(o_dist, rel_table), -1, 1).reshape(q_chunk, N_KV_HEADS, GQA, MOBA_BLOCK)
        o_log = jnp.where(o_dist[:, None, None, :] >= 0, o_log + o_bias, -jnp.inf)
        logits = jnp.concatenate(
            [s_log.reshape(b, q_chunk, N_KV_HEADS, GQA, n_sel * MOBA_BLOCK), o_log], axis=-1)
        p = jax.nn.softmax(logits, axis=-1).astype(kv.dtype)
        p_sel = p[..., :n_sel * MOBA_BLOCK].reshape(b, q_chunk, N_KV_HEADS, GQA, n_sel, MOBA_BLOCK)
        out = (jnp.einsum('bqkgnt,bqkntd->bqkgd', p_sel, sel[..., 1, :])
               + jnp.einsum('bqkgt,btkd->bqkgd', p[..., n_sel * MOBA_BLOCK:], own_kv[:, :, 1]))
        return out.reshape(b, q_chunk, N_HEADS, HEAD_DIM)

    nc = sq // q_chunk
    qs = jnp.moveaxis(q.reshape(b, nc, q_chunk, N_HEADS, HEAD_DIM), 1, 0)
    ps = q_pos.reshape(nc, q_chunk)
    out = lax.map(chunk, (qs, ps))
    return jnp.moveaxis(out, 0, 1).reshape(b, sq, N_HEADS, HEAD_DIM)


def mixer_moba(hp, hs, past_kv, w_qkv, w_o, rel_table):
    tail = (2, N_KV_HEADS, HEAD_DIM)
    qp, kvp = project_qkv(hp, w_qkv, tail)
    qs, kvs = project_qkv(hs, w_qkv, tail)
    s, t = hp.shape[1], hs.shape[1]
    past_len = past_kv.shape[1]
    op = moba_attention(qp, kvp, jnp.arange(s, dtype=jnp.int32), rel_table, MOBA_Q_CHUNK)
    kv_all = jnp.concatenate([past_kv, kvs], axis=1)
    os_ = moba_attention(qs, kv_all, past_len + jnp.arange(t, dtype=jnp.int32), rel_table, 1)
    return out_proj(op, w_o), out_proj(os_, w_o), kvp, kvs


def dilated_prompt(q, kv, dil, rel_table):
    b, s = q.shape[:2]
    m = s // dil
    mb = -(-m // DIL_BAND)
    mp = mb * DIL_BAND

    def to_sub(x):
        x = jnp.moveaxis(x.reshape((b, m, dil) + x.shape[2:]), 2, 1)
        x = jnp.pad(x, [(0, 0), (0, 0), (0, mp - m)] + [(0, 0)] * (x.ndim - 3))
        return x.reshape((b, dil, mb, DIL_BAND) + x.shape[3:])

    qs = to_sub(q.reshape(b, s, DIL_KV_HEADS, DIL_GQA, HEAD_DIM))
    kvs = to_sub(kv)
    prev = jnp.pad(kvs, ((0, 0), (0, 0), (1, 0), (0, 0), (0, 0), (0, 0), (0, 0)))[:, :, :-1]
    band = jnp.concatenate([prev, kvs], axis=3)
    logits = jnp.einsum('brnqkgd,brnskd->brnkgqs', qs, band[:, :, :, :, 0]).astype(jnp.float32)
    steps = (jnp.arange(DIL_BAND)[:, None] + DIL_BAND) - jnp.arange(2 * DIL_BAND)[None, :]
    sub_pos = jnp.arange(mb)[:, None, None] * DIL_BAND - DIL_BAND + jnp.arange(2 * DIL_BAND)[None, None, :]
    keep = (steps >= 0) & (steps <= DIL_BAND) & (sub_pos >= 0)
    bias = jnp.transpose(rel_bias(steps * dil, rel_table), (2, 0, 1)).reshape(
        DIL_KV_HEADS, DIL_GQA, DIL_BAND, 2 * DIL_BAND)
    logits = jnp.where(keep[None, None, :, None, None], logits + bias, -jnp.inf)
    lse = jax.nn.logsumexp(logits, axis=-1)
    p = jnp.exp(logits - lse[..., None]).astype(kv.dtype)
    out = jnp.einsum('brnkgqs,brnskd->brnqkgd', p, band[:, :, :, :, 1])
    out = jnp.moveaxis(out.reshape(b, dil, mp, N_HEADS, HEAD_DIM)[:, :, :m], 1, 2).reshape(b, s, N_HEADS, HEAD_DIM)
    lse = jnp.moveaxis(lse, -1, 3).reshape(b, dil, mp, N_HEADS)[:, :, :m]
    lse = jnp.moveaxis(lse, 1, 2).reshape(b, s, N_HEADS)
    return out, lse


def dilated_sample(q, kv_new, buf, dil, window, rel_table):
    b, t = q.shape[:2]
    lb = buf.shape[1]
    kk = jnp.concatenate([buf, kv_new], axis=1)
    k_steps = jnp.arange(DIL_BAND + 1)
    idx = lb + jnp.arange(t)[:, None] - dil * k_steps[None, :]
    keep = idx >= 0
    g = kk[:, jnp.maximum(idx, 0)]
    qg = q.reshape(b, t, DIL_KV_HEADS, DIL_GQA, HEAD_DIM)
    logits = jnp.einsum('btkgd,btjkd->btkgj', qg, g[:, :, :, 0]).astype(jnp.float32)
    bias = rel_bias(dil * k_steps, rel_table).T.reshape(DIL_KV_HEADS, DIL_GQA, DIL_BAND + 1)
    logits = jnp.where(keep[None, :, None, None, :], logits + bias, -jnp.inf)
    lse = jax.nn.logsumexp(logits, axis=-1)
    p = jnp.exp(logits - lse[..., None]).astype(kk.dtype)
    out = jnp.einsum('btkgj,btjkd->btkgd', p, g[:, :, :, 1]).reshape(b, t, N_HEADS, HEAD_DIM)
    new_len = min(window, lb + t)
    return out, lse.reshape(b, t, N_HEADS), kk[:, lb + t - new_len:]


def merge_by_denominator(outs, lses):
    wts = jax.nn.softmax(jnp.stack(lses), axis=0)
    out = jnp.sum(wts[..., None] * jnp.stack(outs).astype(jnp.float32), axis=0)
    return out.astype(outs[0].dtype)


def mixer_dilated(hp, hs, bufs, w_qkv, w_o, rel_table):
    tail = (N_DIL, 2, DIL_KV_HEADS, HEAD_DIM)
    qp, kvp = project_qkv(hp, w_qkv, tail)
    qs, kvs = project_qkv(hs, w_qkv, tail)
    s = hp.shape[1]
    op_l, lp_l, os_l, ls_l, st_p, st_s = [], [], [], [], [], []
    for g, (window, dil) in enumerate(DIL_PAIRS):
        o, l = dilated_prompt(qp, kvp[:, :, g], dil, rel_table)
        op_l.append(o)
        lp_l.append(l)
        o, l, nbuf = dilated_sample(qs, kvs[:, :, g], bufs[g], dil, window, rel_table)
        os_l.append(o)
        ls_l.append(l)
        st_s.append(nbuf)
        st_p.append(kvp[:, s - min(window, s):, g])
    yp = out_proj(merge_by_denominator(op_l, lp_l), w_o)
    ys = out_proj(merge_by_denominator(os_l, ls_l), w_o)
    return yp, ys, st_p, st_s


def moe_ffn(h, router_w, router_bias, w_gate, w_up, w_down):
    n = h.shape[0]
    scores = jax.nn.sigmoid((h @ router_w).astype(jnp.float32))
    biased = scores + router_bias.astype(jnp.float32)
    grp = lax.top_k(biased.reshape(n, N_EXPERT_GROUPS, EXPERTS_PER_GROUP), GROUP_SCORE_TOPK)[0].sum(-1)
    _, gi = lax.top_k(grp, TOP_GROUPS)
    gmask = jnp.any(gi[..., None] == jnp.arange(N_EXPERT_GROUPS), axis=1)
    emask = jnp.repeat(gmask, EXPERTS_PER_GROUP, axis=1)
    _, ei = lax.top_k(jnp.where(emask, biased, -jnp.inf), TOP_K)
    wsel = jnp.take_along_axis(scores, ei, axis=1)
    wsel = wsel / jnp.sum(wsel, axis=-1, keepdims=True)
    combine = jnp.sum(jax.nn.one_hot(ei, N_EXPERTS, dtype=jnp.float32) * wsel[..., None], axis=1)
    hg = jnp.einsum('nd,edf->nef', h, w_gate)
    hu = jnp.einsum('nd,edf->nef', h, w_up)
    act = jax.nn.silu(hg) * hu * combine[..., None].astype(h.dtype)
    return jnp.einsum('nef,efd->nd', act, w_down)


def setup_inputs(seed: int = 0) -> dict:
    key = jax.random.key(seed)
    ks = jax.random.split(key, 28)
    f32 = jnp.float32

    def nrm(k, shape, scale=1.0):
        return jax.random.normal(k, shape, f32) * scale

    n_pages = PAST_LEN // PAGE_SIZE
    n_used = DEC_BATCH * n_pages
    n_pool = n_used + max(1, n_used // 4)
    page_table = jax.random.permutation(ks[5], n_pool)[:n_used].reshape(DEC_BATCH, n_pages).astype(jnp.int32)
    qd = N_HEADS * HEAD_DIM
    attn_cols = (N_HEADS + 2 * N_KV_HEADS) * HEAD_DIM
    dil_cols = (N_HEADS + N_DIL * 2 * DIL_KV_HEADS) * HEAD_DIM
    d_in = D_MODEL ** -0.5
    o_scale = DN_BETA * qd ** -0.5
    return {
        'x_prompt': nrm(ks[0], (BATCH, SEQ, D_MODEL)),
        'x_sample': nrm(ks[1], (DEC_BATCH, DEC_SEQ, D_MODEL)),
        'cache_a_kv': nrm(ks[2], (N_A, DEC_BATCH, min(SWA_WINDOW, PAST_LEN), 2, N_KV_HEADS, HEAD_DIM)),
        'cache_b_kv_pages': nrm(ks[3], (N_B, n_pool, PAGE_SIZE, 2, N_KV_HEADS, HEAD_DIM)),
        'cache_c_kv_g0': nrm(ks[4], (N_C, DEC_BATCH, min(DIL_PAIRS[0][0], PAST_LEN), 2, DIL_KV_HEADS, HEAD_DIM)),
        'cache_c_kv_g1': nrm(ks[6], (N_C, DEC_BATCH, min(DIL_PAIRS[1][0], PAST_LEN), 2, DIL_KV_HEADS, HEAD_DIM)),
        'cache_c_kv_g2': nrm(ks[7], (N_C, DEC_BATCH, min(DIL_PAIRS[2][0], PAST_LEN), 2, DIL_KV_HEADS, HEAD_DIM)),
        'page_table': page_table,
        'c_prompt': nrm(ks[8], (BATCH, D_MODEL)),
        'c_sample': nrm(ks[9], (DEC_BATCH, D_MODEL)),
        'rel_table': nrm(ks[10], (REL_BUCKETS, N_HEADS), 0.5),
        'ada_w': nrm(ks[11], (DEPTH, D_MODEL, 6 * D_MODEL), d_in),
        'ada_b': nrm(ks[12], (DEPTH, 6 * D_MODEL), 0.02),
        'ln_g': 1.0 + nrm(ks[13], (DEPTH, 2, D_MODEL), 0.02),
        'ln_b': nrm(ks[14], (DEPTH, 2, D_MODEL), 0.02),
        'a_w_qkv': nrm(ks[15], (N_A, D_MODEL, attn_cols), d_in),
        'a_w_o': nrm(ks[16], (N_A, qd, D_MODEL), o_scale),
        'a_sinks': nrm(ks[17], (N_A, N_HEADS), 0.5),
        'b_w_qkv': nrm(ks[18], (N_B, D_MODEL, attn_cols), d_in),
        'b_w_o': nrm(ks[19], (N_B, qd, D_MODEL), o_scale),
        'c_w_qkv': nrm(ks[20], (N_C, D_MODEL, dil_cols), d_in),
        'c_w_o': nrm(ks[21], (N_C, qd, D_MODEL), o_scale),
        'router_w': nrm(ks[22], (D_MODEL, N_EXPERTS), d_in),
        'router_bias': nrm(ks[23], (N_EXPERTS,), 0.01),
        'moe_w_gate': nrm(ks[24], (DEPTH, N_EXPERTS, D_MODEL, D_EXPERT), d_in),
        'moe_w_up': nrm(ks[25], (DEPTH, N_EXPERTS, D_MODEL, D_EXPERT), d_in),
        'moe_w_down': nrm(ks[26], (DEPTH, N_EXPERTS, D_EXPERT, D_MODEL), DN_BETA * D_EXPERT ** -0.5),
    }


def reference(x_prompt, x_sample, cache_a_kv, cache_b_kv_pages, cache_c_kv_g0, cache_c_kv_g1, cache_c_kv_g2,
              page_table, c_prompt, c_sample, rel_table, ada_w, ada_b, ln_g, ln_b,
              a_w_qkv, a_w_o, a_sinks, b_w_qkv, b_w_o, c_w_qkv, c_w_o,
              router_w, router_bias, moe_w_gate, moe_w_up, moe_w_down):
    xp, xs = x_prompt, x_sample
    a_p, a_s, b_p, b_s = [], [], [], []
    c_p = [[] for _ in range(N_DIL)]
    c_s = [[] for _ in range(N_DIL)]
    for i in range(DEPTH):
        kind, slot = LAYER_KIND[i], LAYER_SLOT[i]
        mp_ = adaln(c_prompt, ada_w[i], ada_b[i])
        ms_ = adaln(c_sample, ada_w[i], ada_b[i])
        hp = modulate(xp, mp_[0], mp_[1])
        hs = modulate(xs, ms_[0], ms_[1])
        if kind == 0:
            yp, ys, sp, ss = mixer_swa(hp, hs, cache_a_kv[slot], a_w_qkv[slot], a_w_o[slot], a_sinks[slot], rel_table)
            a_p.append(sp)
            a_s.append(ss)
        elif kind == 1:
            past = cache_b_kv_pages[slot, page_table]
            past = past.reshape((past.shape[0], past.shape[1] * past.shape[2]) + past.shape[3:])
            yp, ys, sp, ss = mixer_moba(hp, hs, past, b_w_qkv[slot], b_w_o[slot], rel_table)
            b_p.append(sp)
            b_s.append(ss)
        else:
            bufs = (cache_c_kv_g0[slot], cache_c_kv_g1[slot], cache_c_kv_g2[slot])
            yp, ys, sp, ss = mixer_dilated(hp, hs, bufs, c_w_qkv[slot], c_w_o[slot], rel_table)
            for g in range(N_DIL):
                c_p[g].append(sp[g])
                c_s[g].append(ss[g])
        xp = layer_norm(DN_ALPHA * xp + mp_[2] * yp, ln_g[i, 0], ln_b[i, 0])
        xs = layer_norm(DN_ALPHA * xs + ms_[2] * ys, ln_g[i, 0], ln_b[i, 0])
        hp = modulate(xp, mp_[3], mp_[4])
        hs = modulate(xs, ms_[3], ms_[4])
        fp = moe_ffn(hp.reshape(-1, D_MODEL), router_w, router_bias, moe_w_gate[i], moe_w_up[i], moe_w_down[i]).reshape(hp.shape)
        fs = moe_ffn(hs.reshape(-1, D_MODEL), router_w, router_bias, moe_w_gate[i], moe_w_up[i], moe_w_down[i]).reshape(hs.shape)
        xp = layer_norm(DN_ALPHA * xp + mp_[5] * fp, ln_g[i, 1], ln_b[i, 1])
        xs = layer_norm(DN_ALPHA * xs + ms_[5] * fs, ln_g[i, 1], ln_b[i, 1])
    return (xp, xs, jnp.stack(a_p), jnp.stack(a_s), jnp.stack(b_p), jnp.stack(b_s),
            jnp.stack(c_p[0]), jnp.stack(c_s[0]), jnp.stack(c_p[1]), jnp.stack(c_s[1]),
            jnp.stack(c_p[2]), jnp.stack(c_s[2]))
```

```python
import functools
import math

import jax
import jax.numpy as jnp
from jax import lax
from jax.experimental import pallas as pl
from jax.experimental.pallas import tpu as pltpu

F32 = jnp.float32
BF16 = jnp.bfloat16
HIGHEST = lax.Precision.HIGHEST

HEAD_DIM = 128
GQA = 4
SWA_WINDOW = 128
MOBA_BLOCK = 256
MOBA_TOPK = 3
MOBA_Q_CHUNK = 8
DIL_PAIRS = ((128, 1), (512, 4), (2048, 16))
DIL_BAND = 128
REL_BUCKETS = 32
REL_MAX_DISTANCE = 2048
N_EXPERT_GROUPS = 4
LN_EPS = 1e-5
LAYER_KINDS = 3

VMEM_LIMIT = 56 * 1024 * 1024


def _params(sem, vmem=VMEM_LIMIT):
    return pltpu.CompilerParams(dimension_semantics=sem, vmem_limit_bytes=vmem)


def _adaln_body(c_ref, w_ref, b_ref, o_ref):
    c = c_ref[...]
    a = c * jax.nn.sigmoid(c)
    o_ref[...] = jnp.dot(a.astype(BF16), w_ref[...].astype(BF16), preferred_element_type=F32) + b_ref[...]


def adaln_all(c, ada_w, ada_b, tn=512):
    n_layers, d, n = ada_w.shape
    tn = min(tn, n)
    r = c.shape[0]
    return pl.pallas_call(
        _adaln_body,
        out_shape=jax.ShapeDtypeStruct((n_layers, r, n), F32),
        grid=(n_layers, n // tn),
        in_specs=[
            pl.BlockSpec((r, d), lambda l, j: (0, 0)),
            pl.BlockSpec((None, d, tn), lambda l, j: (l, 0, j)),
            pl.BlockSpec((None, 1, tn), lambda l, j: (l, 0, j)),
        ],
        out_specs=pl.BlockSpec((None, r, tn), lambda l, j: (l, 0, j)),
        compiler_params=_params(("arbitrary", "arbitrary")),
        name="adaln",
    )(c, ada_w, ada_b.reshape(n_layers, 1, n))


def _proj_body(x_ref, w_ref, o_ref, wbf_ref, *, scale):
    @pl.when(pl.program_id(1) == 0)
    def _():
        wbf_ref[...] = w_ref[...].astype(BF16)

    acc = jnp.dot(x_ref[...], wbf_ref[...], preferred_element_type=F32)
    if scale != 1.0:
        acc = acc * scale
    o_ref[...] = acc.astype(o_ref.dtype)


def project(x, w, col0, ncols, out_dtype, scale=1.0, tn=512):
    m, k = x.shape
    tm = min(m, 512)
    tn = min(tn, ncols)
    assert m % tm == 0 and ncols % tn == 0 and col0 % tn == 0
    j0 = col0 // tn
    return pl.pallas_call(
        functools.partial(_proj_body, scale=scale),
        out_shape=jax.ShapeDtypeStruct((m, ncols), out_dtype),
        grid=(ncols // tn, m // tm),
        in_specs=[
            pl.BlockSpec((tm, k), lambda j, i: (i, 0)),
            pl.BlockSpec((k, tn), lambda j, i: (0, j + j0)),
        ],
        out_specs=pl.BlockSpec((tm, tn), lambda j, i: (i, j)),
        scratch_shapes=[pltpu.VMEM((k, tn), BF16)],
        compiler_params=_params(("arbitrary", "arbitrary")),
        name="project",
    )(x, w)


def _layer_norm(v, g, b):
    mu = jnp.mean(v, axis=-1, keepdims=True)
    d = v - mu
    var = jnp.mean(d * d, axis=-1, keepdims=True)
    return d * lax.rsqrt(var + LN_EPS) * g + b


def _modulate_body(x_ref, shift_ref, scale_ref, h_ref):
    h_ref[...] = (x_ref[...] * (1.0 + scale_ref[...]) + shift_ref[...]).astype(h_ref.dtype)


def _mod_specs(mods, parts, tm, rows_per_batch, d):
    if mods.ndim == 3:
        tiles_per_batch = rows_per_batch // tm
        return [pl.BlockSpec((None, 1, d), functools.partial(
            lambda i, *_, p: (i // tiles_per_batch, 0, p), p=p)) for p in parts]
    return [pl.BlockSpec((tm, d), functools.partial(lambda i, *_, p: (i, p), p=p)) for p in parts]


def modulate(x, mods, rows_per_batch, tm=128):
    m, d = x.shape
    tm = min(tm, m)
    row = pl.BlockSpec((tm, d), lambda i: (i, 0))
    return pl.pallas_call(
        _modulate_body,
        out_shape=jax.ShapeDtypeStruct((m, d), BF16),
        grid=(m // tm,),
        in_specs=[row] + _mod_specs(mods, (0, 1), tm, rows_per_batch, d),
        out_specs=row,
        compiler_params=_params(("arbitrary",)),
        name="modulate",
    )(x, mods, mods)


def _post_attn_body(x_ref, y_ref, gate_ref, shift_ref, scale_ref, g_ref, b_ref, rw_ref,
                    xo_ref, h_ref, lg_ref, *, alpha):
    xm = _layer_norm(alpha * x_ref[...] + gate_ref[...] * y_ref[...], g_ref[...], b_ref[...])
    xo_ref[...] = xm
    h = xm * (1.0 + scale_ref[...]) + shift_ref[...]
    h_ref[...] = h
    lg_ref[...] = jnp.dot(h.astype(BF16), rw_ref[...].astype(BF16), preferred_element_type=F32)


def post_attention(x, y, mods, rows_per_batch, ln_g, ln_b, router_w, alpha, tm=128):
    m, d = x.shape
    tm = min(tm, m)
    n_exp = router_w.shape[1]
    row = pl.BlockSpec((tm, d), lambda i: (i, 0))
    vec = pl.BlockSpec((1, d), lambda i: (0, 0))
    return pl.pallas_call(
        functools.partial(_post_attn_body, alpha=alpha),
        out_shape=(jax.ShapeDtypeStruct((m, d), F32), jax.ShapeDtypeStruct((m, d), F32),
                   jax.ShapeDtypeStruct((m, n_exp), F32)),
        grid=(m // tm,),
        in_specs=[row, row] + _mod_specs(mods, (2, 3, 4), tm, rows_per_batch, d) + [
            vec, vec, pl.BlockSpec((d, n_exp), lambda i: (0, 0))],
        out_specs=(row, row, pl.BlockSpec((tm, n_exp), lambda i: (i, 0))),
        compiler_params=_params(("arbitrary",)),
        name="post_attention",
    )(x, y, mods, mods, mods, ln_g.reshape(1, d), ln_b.reshape(1, d), router_w)


def _post_moe_body(s0_ref, s1_ref, x_ref, gate_ref, g_ref, b_ref, *rest, alpha, tm, tok0, has_next):
    if has_next:
        shift_ref, scale_ref, slots_hbm, xo_ref, hn_ref, buf, sem = rest
    else:
        slots_hbm, xo_ref, buf, sem = rest
    base = tok0 + pl.program_id(0) * tm

    def copies(r):
        return (pltpu.make_async_copy(slots_hbm.at[pl.ds(s0_ref[base + r], 1)], buf.at[0, pl.ds(r, 1)], sem.at[0]),
                pltpu.make_async_copy(slots_hbm.at[pl.ds(s1_ref[base + r], 1)], buf.at[1, pl.ds(r, 1)], sem.at[1]))

    def issue(r, c):
        a, b = copies(r)
        a.start()
        b.start()
        return c

    def wait(r, c):
        a, b = copies(r)
        a.wait()
        b.wait()
        return c

    lax.fori_loop(0, tm, issue, 0)
    lax.fori_loop(0, tm, wait, 0)
    f = buf[0] + buf[1]
    xn = _layer_norm(alpha * x_ref[...] + gate_ref[...] * f, g_ref[...], b_ref[...])
    xo_ref[...] = xn
    if has_next:
        hn_ref[...] = (xn * (1.0 + scale_ref[...]) + shift_ref[...]).astype(hn_ref.dtype)


def post_moe(x, slots, s0, s1, tok0, mods, mods_next, rows_per_batch, ln_g, ln_b, alpha, tm=128):
    m, d = x.shape
    tm = min(tm, m)
    has_next = mods_next is not None
    row = pl.BlockSpec((tm, d), lambda i, *_: (i, 0))
    vec = pl.BlockSpec((1, d), lambda i, *_: (0, 0))
    in_specs = [row] + _mod_specs(mods, (5,), tm, rows_per_batch, d) + [vec, vec]
    args = [x, mods, ln_g.reshape(1, d), ln_b.reshape(1, d)]
    out_shape = [jax.ShapeDtypeStruct((m, d), F32)]
    out_specs = [row]
    if has_next:
        in_specs += _mod_specs(mods_next, (0, 1), tm, rows_per_batch, d)
        args += [mods_next, mods_next]
        out_shape.append(jax.ShapeDtypeStruct((m, d), BF16))
        out_specs.append(row)
    in_specs.append(pl.BlockSpec(memory_space=pl.ANY))
    args.append(slots)
    out = pl.pallas_call(
        functools.partial(_post_moe_body, alpha=alpha, tm=tm, tok0=tok0, has_next=has_next),
        out_shape=tuple(out_shape),
        grid_spec=pltpu.PrefetchScalarGridSpec(
            num_scalar_prefetch=2, grid=(m // tm,), in_specs=in_specs, out_specs=tuple(out_specs),
            scratch_shapes=[pltpu.VMEM((2, tm, d), F32), pltpu.SemaphoreType.DMA((2,))]),
        compiler_params=_params(("arbitrary",)),
        name="post_moe",
    )(s0, s1, *args)
    return out if has_next else (out[0], None)


def _route_body(lg_ref, bias_ref, ids_ref, w_ref, *, per_group):
    s = jax.nn.sigmoid(lg_ref[...])
    bz = s + bias_ref[...]
    n_exp = s.shape[0]
    rows_b = [bz[e:e + 1, :] for e in range(n_exp)]
    rows_s = [s[e:e + 1, :] for e in range(n_exp)]
    best = None
    for g in range(n_exp // per_group):
        members = range(g * per_group, (g + 1) * per_group)
        rank = {}
        for e in members:
            r = jnp.zeros_like(rows_b[e], dtype=jnp.int32)
            for o in members:
                if o == e:
                    continue
                ahead = (rows_b[o] > rows_b[e]) | ((rows_b[o] == rows_b[e]) & (o < e))
                r = r + ahead.astype(jnp.int32)
            rank[e] = r
        zero = jnp.zeros_like(rows_b[0])
        score = sum(jnp.where(rank[e] < 2, rows_b[e], zero) for e in members)
        e0 = sum(jnp.where(rank[e] == 0, e, 0) for e in members)
        e1 = sum(jnp.where(rank[e] == 1, e, 0) for e in members)
        s0 = sum(jnp.where(rank[e] == 0, rows_s[e], zero) for e in members)
        s1 = sum(jnp.where(rank[e] == 1, rows_s[e], zero) for e in members)
        if best is None:
            best = (score, e0, e1, s0, s1)
        else:
            take = score > best[0]
            best = tuple(jnp.where(take, new, old) for new, old in zip((score, e0, e1, s0, s1), best))
    _, e0, e1, s0, s1 = best
    tot = s0 + s1
    ids_ref[0:1, :] = e0
    ids_ref[1:2, :] = e1
    w_ref[0:1, :] = s0 / tot
    w_ref[1:2, :] = s1 / tot


def route(logits_t, router_bias):
    n_exp, n = logits_t.shape
    return pl.pallas_call(
        functools.partial(_route_body, per_group=n_exp // N_EXPERT_GROUPS),
        out_shape=(jax.ShapeDtypeStruct((2, n), jnp.int32), jax.ShapeDtypeStruct((2, n), F32)),
        name="route",
    )(logits_t, router_bias.reshape(n_exp, 1).astype(F32))


def _dispatch_body(tok_ref, nused_ref, src_hbm, o_ref, buf, sem, *, tm):
    i = pl.program_id(0)

    @pl.when(i < nused_ref[0])
    def _():
        base = i * tm

        def copy(r):
            return pltpu.make_async_copy(src_hbm.at[pl.ds(tok_ref[base + r], 1)], buf.at[pl.ds(r, 1)], sem)

        def issue(r, c):
            copy(r).start()
            return c

        def wait(r, c):
            copy(r).wait()
            return c

        lax.fori_loop(0, tm, issue, 0)
        lax.fori_loop(0, tm, wait, 0)
        o_ref[...] = buf[...].astype(o_ref.dtype)

    @pl.when(i >= nused_ref[0])
    def _():
        o_ref[...] = jnp.zeros_like(o_ref)


def dispatch(h, slot_tok, n_used, n_slots, tm):
    n, d = h.shape
    return pl.pallas_call(
        functools.partial(_dispatch_body, tm=tm),
        out_shape=jax.ShapeDtypeStruct((n_slots, d), BF16),
        grid_spec=pltpu.PrefetchScalarGridSpec(
            num_scalar_prefetch=2, grid=(n_slots // tm,),
            in_specs=[pl.BlockSpec(memory_space=pl.ANY)],
            out_specs=pl.BlockSpec((tm, d), lambda i, tok, nu: (i, 0)),
            scratch_shapes=[pltpu.VMEM((tm, d), F32), pltpu.SemaphoreType.DMA(())]),
        compiler_params=_params(("arbitrary",)),
        name="dispatch",
    )(slot_tok, n_used, h)


def _gate_up_body(te_ref, first_ref, nused_ref, x_ref, wg_ref, wu_ref, sw_ref, o_ref, wg_bf, wu_bf):
    t = pl.program_id(1)

    @pl.when(first_ref[t] == 1)
    def _():
        wg_bf[...] = wg_ref[...].astype(BF16)
        wu_bf[...] = wu_ref[...].astype(BF16)

    @pl.when(t < nused_ref[0])
    def _():
        x = x_ref[...]
        hg = jnp.dot(x, wg_bf[...], preferred_element_type=F32)
        hu = jnp.dot(x, wu_bf[...], preferred_element_type=F32)
        act = hg * jax.nn.sigmoid(hg) * hu * sw_ref[...]
        o_ref[...] = act.astype(o_ref.dtype)

    @pl.when(t >= nused_ref[0])
    def _():
        o_ref[...] = jnp.zeros_like(o_ref)


def expert_gate_up(xs, w_gate, w_up, slot_w, tile_e, tile_first, n_used, tm, tf=512):
    n_slots, d = xs.shape
    f = w_gate.shape[2]
    tf = min(tf, f)
    n_tiles = n_slots // tm

    def tile(t, nu):
        return jnp.minimum(t, nu[0] - 1)

    w_spec = pl.BlockSpec((None, d, tf), lambda j, t, te, fi, nu: (te[t], 0, j))
    return pl.pallas_call(
        _gate_up_body,
        out_shape=jax.ShapeDtypeStruct((n_slots, f), BF16),
        grid_spec=pltpu.PrefetchScalarGridSpec(
            num_scalar_prefetch=3, grid=(f // tf, n_tiles),
            in_specs=[pl.BlockSpec((tm, d), lambda j, t, te, fi, nu: (tile(t, nu), 0)), w_spec, w_spec,
                      pl.BlockSpec((tm, 1), lambda j, t, te, fi, nu: (tile(t, nu), 0))],
            out_specs=pl.BlockSpec((tm, tf), lambda j, t, te, fi, nu: (t, j)),
            scratch_shapes=[pltpu.VMEM((d, tf), BF16), pltpu.VMEM((d, tf), BF16)]),
        compiler_params=_params(("arbitrary", "arbitrary")),
        name="expert_gate_up",
    )(tile_e, tile_first, n_used, xs, w_gate, w_up, slot_w)


def _down_body(te_ref, first_ref, nused_ref, a_ref, w_ref, o_ref, w_bf):
    t = pl.program_id(1)

    @pl.when(first_ref[t] == 1)
    def _():
        w_bf[...] = w_ref[...].astype(BF16)

    @pl.when(t < nused_ref[0])
    def _():
        o_ref[...] = jnp.dot(a_ref[...], w_bf[...], preferred_element_type=F32)

    @pl.when(t >= nused_ref[0])
    def _():
        o_ref[...] = jnp.zeros_like(o_ref)


def expert_down(act, w_down, tile_e, tile_first, n_used, tm, tn=1024):
    n_slots, f = act.shape
    d = w_down.shape[2]
    tn = min(tn, d)
    n_tiles = n_slots // tm

    def tile(t, nu):
        return jnp.minimum(t, nu[0] - 1)

    return pl.pallas_call(
        _down_body,
        out_shape=jax.ShapeDtypeStruct((n_slots, d), F32),
        grid_spec=pltpu.PrefetchScalarGridSpec(
            num_scalar_prefetch=3, grid=(d // tn, n_tiles),
            in_specs=[pl.BlockSpec((tm, f), lambda j, t, te, fi, nu: (tile(t, nu), 0)),
                      pl.BlockSpec((None, f, tn), lambda j, t, te, fi, nu: (te[t], 0, j))],
            out_specs=pl.BlockSpec((tm, tn), lambda j, t, te, fi, nu: (t, j)),
            scratch_shapes=[pltpu.VMEM((f, tn), BF16)]),
        compiler_params=_params(("arbitrary", "arbitrary")),
        name="expert_down",
    )(tile_e, tile_first, n_used, act, w_down)


def moe_plan(ids, wts, n_exp, tm):
    n = ids.shape[1]
    n_pairs = 2 * n
    n_tiles = -(-(n_pairs + n_exp * (tm - 1)) // tm)
    n_slots = n_tiles * tm
    e_flat = ids.reshape(-1)
    onehot = (e_flat[:, None] == jnp.arange(n_exp, dtype=jnp.int32)[None, :]).astype(jnp.int32)
    csum = jnp.cumsum(onehot, axis=0)
    rank = jnp.sum(onehot * csum, axis=1) - 1
    counts = csum[-1]
    padded = ((counts + tm - 1) // tm) * tm
    ends = jnp.cumsum(padded)
    starts = ends - padded
    slot = (jnp.sum(onehot * starts[None, :], axis=1) + rank).astype(jnp.int32)
    tok = jnp.tile(jnp.arange(n, dtype=jnp.int32), 2)
    slot_tok = jnp.zeros((n_slots,), jnp.int32).at[slot].set(tok)
    slot_w = jnp.zeros((n_slots,), F32).at[slot].set(wts.reshape(-1))
    n_used = (ends[-1] // tm).astype(jnp.int32)
    tile_start = jnp.arange(n_tiles, dtype=jnp.int32) * tm
    tile_start = jnp.minimum(tile_start, ends[-1] - 1)
    tile_e = jnp.minimum(jnp.sum((tile_start[:, None] >= ends[None, :]).astype(jnp.int32), axis=1), n_exp - 1)
    tile_first = jnp.concatenate([jnp.ones((1,), jnp.int32), (tile_e[1:] != tile_e[:-1]).astype(jnp.int32)])
    return dict(n_slots=n_slots, slot_tok=slot_tok, slot_w=slot_w.reshape(n_slots, 1), n_used=n_used.reshape(1),
                tile_e=tile_e.astype(jnp.int32), tile_first=tile_first, s0=slot[:n], s1=slot[n:])


def moe_experts(h_all, logits, router_bias, w_gate, w_up, w_down, tm=256):
    n_exp = w_gate.shape[0]
    ids, wts = route(logits.T, router_bias)
    plan = moe_plan(ids, wts, n_exp, tm)
    xs = dispatch(h_all, plan["slot_tok"], plan["n_used"], plan["n_slots"], tm)
    act = expert_gate_up(xs, w_gate, w_up, plan["slot_w"], plan["tile_e"], plan["tile_first"], plan["n_used"], tm)
    out = expert_down(act, w_down, plan["tile_e"], plan["tile_first"], plan["n_used"], tm)
    return out, plan["s0"], plan["s1"]


def rel_bucket(dist):
    n = jnp.maximum(dist, 0)
    exact = REL_BUCKETS // 2
    scaled = (jnp.log(jnp.maximum(n, exact).astype(F32) / exact)
              / math.log(REL_MAX_DISTANCE / exact) * (REL_BUCKETS - exact))
    large = jnp.minimum(exact + scaled.astype(jnp.int32), REL_BUCKETS - 1)
    return jnp.where(n < exact, n, large)


def rel_bias(dist, rel_table):
    return rel_table.astype(F32)[rel_bucket(dist)]


def sink_softmax(logits, sink):
    m = jnp.maximum(jnp.max(logits, axis=-1, keepdims=True), sink)
    p = jnp.exp(logits - m)
    return p / (jnp.sum(p, axis=-1, keepdims=True) + jnp.exp(sink - m))


def swa_prompt(q, kv, sinks, rel_table):
    b, s = q.shape[:2]
    n_heads = q.shape[2]
    kvh = kv.shape[3]
    gqa = n_heads // kvh
    w = SWA_WINDOW
    nb = s // w
    qb = q.reshape(b, nb, w, kvh, gqa, HEAD_DIM)
    kvb = kv.reshape(b, nb, w, 2, kvh, HEAD_DIM)
    prev = jnp.pad(kvb, ((0, 0), (1, 0), (0, 0), (0, 0), (0, 0), (0, 0)))[:, :-1]
    band = jnp.concatenate([prev, kvb], axis=2)
    logits = jnp.einsum('bnqkgd,bnskd->bnkgqs', qb, band[:, :, :, 0]).astype(F32)
    dist = (jnp.arange(w)[:, None] + w) - jnp.arange(2 * w)[None, :]
    key_pos = jnp.arange(nb)[:, None, None] * w - w + jnp.arange(2 * w)[None, None, :]
    keep = (dist >= 0) & (dist < w) & (key_pos >= 0)
    bias = jnp.transpose(rel_bias(dist, rel_table), (2, 0, 1)).reshape(kvh, gqa, w, 2 * w)
    logits = jnp.where(keep[None, :, None, None], logits + bias, -jnp.inf)
    p = sink_softmax(logits, sinks.astype(F32).reshape(kvh, gqa, 1, 1))
    out = jnp.einsum('bnkgqs,bnskd->bnqkgd', p.astype(kv.dtype), band[:, :, :, 1])
    return out.reshape(b, s, n_heads, HEAD_DIM)


def swa_sample(q, kv_new, buf, sinks, rel_table):
    b, t = q.shape[:2]
    n_heads = q.shape[2]
    kvh = buf.shape[3]
    gqa = n_heads // kvh
    lb = buf.shape[1]
    kk = jnp.concatenate([buf, kv_new], axis=1)
    dist = (lb + jnp.arange(t))[:, None] - jnp.arange(lb + t)[None, :]
    keep = (dist >= 0) & (dist < SWA_WINDOW)
    qg = q.reshape(b, t, kvh, gqa, HEAD_DIM)
    logits = jnp.einsum('btkgd,bskd->bkgts', qg, kk[:, :, 0]).astype(F32)
    bias = jnp.transpose(rel_bias(dist, rel_table), (2, 0, 1)).reshape(kvh, gqa, t, lb + t)
    logits = jnp.where(keep, logits + bias, -jnp.inf)
    p = sink_softmax(logits, sinks.astype(F32).reshape(kvh, gqa, 1, 1))
    out = jnp.einsum('bkgts,bskd->btkgd', p.astype(kk.dtype), kk[:, :, 1]).reshape(b, t, n_heads, HEAD_DIM)
    new_len = min(SWA_WINDOW, lb + t)
    return out, kk[:, lb + t - new_len:]


def moba_attention(q, kv, q_pos, rel_table, q_chunk):
    b, sq = q.shape[:2]
    n_heads = q.shape[2]
    kvh = kv.shape[3]
    gqa = n_heads // kvh
    L = kv.shape[1]
    nb = -(-L // MOBA_BLOCK)
    kvb = jnp.pad(kv, ((0, 0), (0, nb * MOBA_BLOCK - L), (0, 0), (0, 0), (0, 0)))
    kvb = kvb.reshape(b, nb, MOBA_BLOCK, 2, kvh, HEAD_DIM)
    kmean = jnp.mean(kvb[:, :, :, 0], axis=2, dtype=F32)
    n_sel = min(MOBA_TOPK, nb)
    tab = rel_table.astype(F32).T.reshape(kvh, gqa, REL_BUCKETS)
    b_i = jnp.arange(b)[:, None, None, None]
    kv_i = jnp.arange(kvh)[None, None, :, None]
    t_i = jnp.arange(MOBA_BLOCK)

    def chunk(args):
        qc, pc = args
        qc = qc.reshape(b, q_chunk, kvh, gqa, HEAD_DIM)
        own = pc[0] // MOBA_BLOCK
        gate = jnp.einsum('bqkgd,bnkd->bqkn', qc.astype(F32), kmean)
        gate = jnp.where(jnp.arange(nb) < own, gate, -jnp.inf)
        _, blk = lax.top_k(gate, n_sel)
        sel = kvb[b_i, blk, :, :, kv_i]
        s_log = jnp.einsum('bqkgd,bqkntd->bqkgnt', qc, sel[..., 0, :]).astype(F32)
        s_dist = pc[None, :, None, None, None] - (blk[..., None] * MOBA_BLOCK + t_i)
        s_bias = tab[jnp.arange(kvh)[:, None, None, None], jnp.arange(gqa)[:, None, None],
                     rel_bucket(s_dist)[:, :, :, None]]
        s_keep = (blk < own)[:, :, :, None, :, None]
        s_log = jnp.where(s_keep, s_log + s_bias, -jnp.inf)
        own_kv = lax.dynamic_index_in_dim(kvb, own, axis=1, keepdims=False)
        o_log = jnp.einsum('bqkgd,btkd->bqkgt', qc, own_kv[:, :, 0]).astype(F32)
        o_dist = pc[:, None] - (own * MOBA_BLOCK + t_i)[None, :]
        o_bias = jnp.moveaxis(rel_bias(o_dist, rel_table), -1, 1).reshape(q_chunk, kvh, gqa, MOBA_BLOCK)
        o_log = jnp.where(o_dist[:, None, None, :] >= 0, o_log + o_bias, -jnp.inf)
        logits = jnp.concatenate([s_log.reshape(b, q_chunk, kvh, gqa, n_sel * MOBA_BLOCK), o_log], axis=-1)
        p = jax.nn.softmax(logits, axis=-1).astype(kv.dtype)
        p_sel = p[..., :n_sel * MOBA_BLOCK].reshape(b, q_chunk, kvh, gqa, n_sel, MOBA_BLOCK)
        out = (jnp.einsum('bqkgnt,bqkntd->bqkgd', p_sel, sel[..., 1, :])
               + jnp.einsum('bqkgt,btkd->bqkgd', p[..., n_sel * MOBA_BLOCK:], own_kv[:, :, 1]))
        return out.reshape(b, q_chunk, n_heads, HEAD_DIM)

    nc = sq // q_chunk
    qs = jnp.moveaxis(q.reshape(b, nc, q_chunk, n_heads, HEAD_DIM), 1, 0)
    ps = q_pos.reshape(nc, q_chunk)
    out = lax.map(chunk, (qs, ps))
    return jnp.moveaxis(out, 0, 1).reshape(b, sq, n_heads, HEAD_DIM)


def dilated_prompt(q, kv, dil, rel_table):
    b, s = q.shape[:2]
    n_heads = q.shape[2]
    kvh = kv.shape[3]
    gqa = n_heads // kvh
    m = s // dil
    mb = -(-m // DIL_BAND)
    mp = mb * DIL_BAND

    def to_sub(x):
        x = jnp.moveaxis(x.reshape((b, m, dil) + x.shape[2:]), 2, 1)
        x = jnp.pad(x, [(0, 0), (0, 0), (0, mp - m)] + [(0, 0)] * (x.ndim - 3))
        return x.reshape((b, dil, mb, DIL_BAND) + x.shape[3:])

    qs = to_sub(q.reshape(b, s, kvh, gqa, HEAD_DIM))
    kvs = to_sub(kv)
    prev = jnp.pad(kvs, ((0, 0), (0, 0), (1, 0), (0, 0), (0, 0), (0, 0), (0, 0)))[:, :, :-1]
    band = jnp.concatenate([prev, kvs], axis=3)
    logits = jnp.einsum('brnqkgd,brnskd->brnkgqs', qs, band[:, :, :, :, 0]).astype(F32)
    steps = (jnp.arange(DIL_BAND)[:, None] + DIL_BAND) - jnp.arange(2 * DIL_BAND)[None, :]
    sub_pos = jnp.arange(mb)[:, None, None] * DIL_BAND - DIL_BAND + jnp.arange(2 * DIL_BAND)[None, None, :]
    keep = (steps >= 0) & (steps <= DIL_BAND) & (sub_pos >= 0)
    bias = jnp.transpose(rel_bias(steps * dil, rel_table), (2, 0, 1)).reshape(kvh, gqa, DIL_BAND, 2 * DIL_BAND)
    logits = jnp.where(keep[None, None, :, None, None], logits + bias, -jnp.inf)
    lse = jax.nn.logsumexp(logits, axis=-1)
    p = jnp.exp(logits - lse[..., None]).astype(kv.dtype)
    out = jnp.einsum('brnkgqs,brnskd->brnqkgd', p, band[:, :, :, :, 1])
    out = jnp.moveaxis(out.reshape(b, dil, mp, n_heads, HEAD_DIM)[:, :, :m], 1, 2).reshape(b, s, n_heads, HEAD_DIM)
    lse = jnp.moveaxis(lse, -1, 3).reshape(b, dil, mp, n_heads)[:, :, :m]
    lse = jnp.moveaxis(lse, 1, 2).reshape(b, s, n_heads)
    return out, lse


def dilated_sample(q, kv_new, buf, dil, window, rel_table):
    b, t = q.shape[:2]
    n_heads = q.shape[2]
    kvh = buf.shape[3]
    gqa = n_heads // kvh
    lb = buf.shape[1]
    kk = jnp.concatenate([buf, kv_new], axis=1)
    k_steps = jnp.arange(DIL_BAND + 1)
    idx = lb + jnp.arange(t)[:, None] - dil * k_steps[None, :]
    keep = idx >= 0
    g = kk[:, jnp.maximum(idx, 0)]
    qg = q.reshape(b, t, kvh, gqa, HEAD_DIM)
    logits = jnp.einsum('btkgd,btjkd->btkgj', qg, g[:, :, :, 0]).astype(F32)
    bias = rel_bias(dil * k_steps, rel_table).T.reshape(kvh, gqa, DIL_BAND + 1)
    logits = jnp.where(keep[None, :, None, None, :], logits + bias, -jnp.inf)
    lse = jax.nn.logsumexp(logits, axis=-1)
    p = jnp.exp(logits - lse[..., None]).astype(kk.dtype)
    out = jnp.einsum('btkgj,btjkd->btkgd', p, g[:, :, :, 1]).reshape(b, t, n_heads, HEAD_DIM)
    new_len = min(window, lb + t)
    return out, lse.reshape(b, t, n_heads), kk[:, lb + t - new_len:]


def merge_by_denominator(outs, lses):
    wts = jax.nn.softmax(jnp.stack(lses), axis=0)
    return jnp.sum(wts[..., None] * jnp.stack(outs).astype(F32), axis=0)


def kernel(x_prompt, x_sample, cache_a_kv, cache_b_kv_pages, cache_c_kv_g0, cache_c_kv_g1, cache_c_kv_g2,
           page_table, c_prompt, c_sample, rel_table, ada_w, ada_b, ln_g, ln_b,
           a_w_qkv, a_w_o, a_sinks, b_w_qkv, b_w_o, c_w_qkv, c_w_o,
           router_w, router_bias, moe_w_gate, moe_w_up, moe_w_down):
    bp, seq, d = x_prompt.shape
    bs, dseq, _ = x_sample.shape
    depth = ada_w.shape[0]
    n_heads = d // HEAD_DIM
    qd = n_heads * HEAD_DIM
    kvh = n_heads // GQA
    dil_kvh = kvh // 2
    n_dil = len(DIL_PAIRS)
    alpha = (2 * depth) ** 0.25
    q_scale = HEAD_DIM ** -0.5
    mp, ms = bp * seq, bs * dseq
    past_len = page_table.shape[1] * cache_b_kv_pages.shape[2]

    n_c = bp + bs
    c_rows = -(-n_c // 8) * 8
    c_all = jnp.concatenate([c_prompt, c_sample, jnp.zeros((c_rows - n_c, d), F32)], axis=0)
    mods_all = adaln_all(c_all, ada_w, ada_b)

    def layer_mods(i):
        return mods_all[i, :bp].reshape(bp, 1, 6 * d), mods_all[i, bp:n_c]

    xp = x_prompt.reshape(mp, d)
    xs = x_sample.reshape(ms, d)
    mods_p, mods_s = layer_mods(0)
    hp = modulate(xp, mods_p, seq)
    hs = modulate(xs, mods_s, 1)

    a_p, a_s, b_p, b_s = [], [], [], []
    c_p = [[] for _ in range(n_dil)]
    c_s = [[] for _ in range(n_dil)]
    slot_of = [0] * LAYER_KINDS
    for i in range(depth):
        kind = i % LAYER_KINDS
        slot = slot_of[kind]
        slot_of[kind] += 1
        w_qkv, w_o = ((a_w_qkv, a_w_o), (b_w_qkv, b_w_o), (c_w_qkv, c_w_o))[kind]
        w_qkv, w_o = w_qkv[slot], w_o[slot]
        kv_cols = w_qkv.shape[1] - qd
        qp = project(hp, w_qkv, 0, qd, BF16, q_scale).reshape(bp, seq, n_heads, HEAD_DIM)
        kvp = project(hp, w_qkv, qd, kv_cols, F32)
        qs = project(hs, w_qkv, 0, qd, BF16, q_scale).reshape(bs, dseq, n_heads, HEAD_DIM)
        kvs = project(hs, w_qkv, qd, kv_cols, F32)
        if kind == 0:
            kvp = kvp.reshape(bp, seq, 2, kvh, HEAD_DIM)
            kvs = kvs.reshape(bs, dseq, 2, kvh, HEAD_DIM)
            op = swa_prompt(qp.astype(F32), kvp, a_sinks[slot], rel_table)
            os_, new_buf = swa_sample(qs.astype(F32), kvs, cache_a_kv[slot], a_sinks[slot], rel_table)
            a_p.append(kvp[:, seq - min(SWA_WINDOW, seq):])
            a_s.append(new_buf)
        elif kind == 1:
            kvp = kvp.reshape(bp, seq, 2, kvh, HEAD_DIM)
            kvs = kvs.reshape(bs, dseq, 2, kvh, HEAD_DIM)
            past = cache_b_kv_pages[slot, page_table]
            past = past.reshape((past.shape[0], past.shape[1] * past.shape[2]) + past.shape[3:])
            op = moba_attention(qp.astype(F32), kvp, jnp.arange(seq, dtype=jnp.int32), rel_table, MOBA_Q_CHUNK)
            kv_all = jnp.concatenate([past, kvs], axis=1)
            os_ = moba_attention(qs.astype(F32), kv_all, past_len + jnp.arange(dseq, dtype=jnp.int32), rel_table, 1)
            b_p.append(kvp)
            b_s.append(kvs)
        else:
            kvp = kvp.reshape(bp, seq, n_dil, 2, dil_kvh, HEAD_DIM)
            kvs = kvs.reshape(bs, dseq, n_dil, 2, dil_kvh, HEAD_DIM)
            bufs = (cache_c_kv_g0[slot], cache_c_kv_g1[slot], cache_c_kv_g2[slot])
            op_l, lp_l, os_l, ls_l = [], [], [], []
            for g, (window, dil) in enumerate(DIL_PAIRS):
                o, l = dilated_prompt(qp.astype(F32), kvp[:, :, g], dil, rel_table)
                op_l.append(o)
                lp_l.append(l)
                o, l, nbuf = dilated_sample(qs.astype(F32), kvs[:, :, g], bufs[g], dil, window, rel_table)
                os_l.append(o)
                ls_l.append(l)
                c_s[g].append(nbuf)
                c_p[g].append(kvp[:, seq - min(window, seq):, g])
            op = merge_by_denominator(op_l, lp_l)
            os_ = merge_by_denominator(os_l, ls_l)
        yp = project(op.reshape(mp, qd).astype(BF16), w_o, 0, d, F32)
        ys = project(os_.reshape(ms, qd).astype(BF16), w_o, 0, d, F32)

        xp, h2p, lgp = post_attention(xp, yp, mods_p, seq, ln_g[i, 0], ln_b[i, 0], router_w, alpha)
        xs, h2s, lgs = post_attention(xs, ys, mods_s, 1, ln_g[i, 0], ln_b[i, 0], router_w, alpha)
        slots, s0, s1 = moe_experts(jnp.concatenate([h2p, h2s], axis=0), jnp.concatenate([lgp, lgs], axis=0),
                                    router_bias, moe_w_gate[i], moe_w_up[i], moe_w_down[i])
        if i + 1 < depth:
            nxt_p, nxt_s = layer_mods(i + 1)
        else:
            nxt_p = nxt_s = None
        xp, hp = post_moe(xp, slots, s0, s1, 0, mods_p, nxt_p, seq, ln_g[i, 1], ln_b[i, 1], alpha)
        xs, hs = post_moe(xs, slots, s0, s1, mp, mods_s, nxt_s, 1, ln_g[i, 1], ln_b[i, 1], alpha)
        mods_p, mods_s = nxt_p, nxt_s

    return (xp.reshape(bp, seq, d), xs.reshape(bs, dseq, d), jnp.stack(a_p), jnp.stack(a_s),
            jnp.stack(b_p), jnp.stack(b_s), jnp.stack(c_p[0]), jnp.stack(c_s[0]), jnp.stack(c_p[1]),
            jnp.stack(c_s[1]), jnp.stack(c_p[2]), jnp.stack(c_s[2]))
```

```python
import functools
import math

import jax
import jax.numpy as jnp
from jax import lax
from jax.experimental import pallas as pl
from jax.experimental.pallas import tpu as pltpu

F32 = jnp.float32
BF16 = jnp.bfloat16
HIGHEST = lax.Precision.HIGHEST

HEAD_DIM = 128
GQA = 4
SWA_WINDOW = 128
MOBA_BLOCK = 256
MOBA_TOPK = 3
MOBA_Q_CHUNK = 8
DIL_PAIRS = ((128, 1), (512, 4), (2048, 16))
DIL_BAND = 128
REL_BUCKETS = 32
REL_MAX_DISTANCE = 2048
N_EXPERT_GROUPS = 4
LN_EPS = 1e-5
LAYER_KINDS = 3

VMEM_LIMIT = 56 * 1024 * 1024


def _params(sem, vmem=VMEM_LIMIT):
    return pltpu.CompilerParams(dimension_semantics=sem, vmem_limit_bytes=vmem)


def _adaln_body(c_ref, w_ref, b_ref, o_ref):
    c = c_ref[...]
    a = c * jax.nn.sigmoid(c)
    o_ref[...] = jnp.dot(a.astype(BF16), w_ref[...].astype(BF16), preferred_element_type=F32) + b_ref[...]


def adaln_all(c, ada_w, ada_b, tn=512):
    n_layers, d, n = ada_w.shape
    tn = min(tn, n)
    r = c.shape[0]
    return pl.pallas_call(
        _adaln_body,
        out_shape=jax.ShapeDtypeStruct((n_layers, r, n), F32),
        grid=(n_layers, n // tn),
        in_specs=[
            pl.BlockSpec((r, d), lambda l, j: (0, 0)),
            pl.BlockSpec((None, d, tn), lambda l, j: (l, 0, j)),
            pl.BlockSpec((None, 1, tn), lambda l, j: (l, 0, j)),
        ],
        out_specs=pl.BlockSpec((None, r, tn), lambda l, j: (l, 0, j)),
        compiler_params=_params(("arbitrary", "arbitrary")),
        name="adaln",
    )(c, ada_w, ada_b.reshape(n_layers, 1, n))


def _proj_body(x_ref, w_ref, o_ref, wbf_ref, *, scale):
    @pl.when(pl.program_id(1) == 0)
    def _():
        wbf_ref[...] = w_ref[...].astype(BF16)

    acc = jnp.dot(x_ref[...], wbf_ref[...], preferred_element_type=F32)
    if scale != 1.0:
        acc = acc * scale
    o_ref[...] = acc.astype(o_ref.dtype)


def project(x, w, col0, ncols, out_dtype, scale=1.0, tn=512):
    m, k = x.shape
    tm = min(m, 512)
    tn = min(tn, ncols)
    assert m % tm == 0 and ncols % tn == 0 and col0 % tn == 0
    j0 = col0 // tn
    return pl.pallas_call(
        functools.partial(_proj_body, scale=scale),
        out_shape=jax.ShapeDtypeStruct((m, ncols), out_dtype),
        grid=(ncols // tn, m // tm),
        in_specs=[
            pl.BlockSpec((tm, k), lambda j, i: (i, 0)),
            pl.BlockSpec((k, tn), lambda j, i: (0, j + j0)),
        ],
        out_specs=pl.BlockSpec((tm, tn), lambda j, i: (i, j)),
        scratch_shapes=[pltpu.VMEM((k, tn), BF16)],
        compiler_params=_params(("arbitrary", "arbitrary")),
        name="project",
    )(x, w)


def _layer_norm(v, g, b):
    mu = jnp.mean(v, axis=-1, keepdims=True)
    d = v - mu
    var = jnp.mean(d * d, axis=-1, keepdims=True)
    return d * lax.rsqrt(var + LN_EPS) * g + b


def _modulate_body(x_ref, shift_ref, scale_ref, h_ref):
    h_ref[...] = (x_ref[...] * (1.0 + scale_ref[...]) + shift_ref[...]).astype(h_ref.dtype)


def _mod_specs(mods, parts, tm, rows_per_batch, d):
    if mods.ndim == 3:
        tiles_per_batch = rows_per_batch // tm
        return [pl.BlockSpec((None, 1, d), functools.partial(
            lambda i, *_, p: (i // tiles_per_batch, 0, p), p=p)) for p in parts]
    return [pl.BlockSpec((tm, d), functools.partial(lambda i, *_, p: (i, p), p=p)) for p in parts]


def modulate(x, mods, rows_per_batch, tm=128):
    m, d = x.shape
    tm = min(tm, m)
    row = pl.BlockSpec((tm, d), lambda i: (i, 0))
    return pl.pallas_call(
        _modulate_body,
        out_shape=jax.ShapeDtypeStruct((m, d), BF16),
        grid=(m // tm,),
        in_specs=[row] + _mod_specs(mods, (0, 1), tm, rows_per_batch, d),
        out_specs=row,
        compiler_params=_params(("arbitrary",)),
        name="modulate",
    )(x, mods, mods)


def _post_attn_body(x_ref, y_ref, gate_ref, shift_ref, scale_ref, g_ref, b_ref, rw_ref,
                    xo_ref, h_ref, lg_ref, *, alpha):
    xm = _layer_norm(alpha * x_ref[...] + gate_ref[...] * y_ref[...], g_ref[...], b_ref[...])
    xo_ref[...] = xm
    h = xm * (1.0 + scale_ref[...]) + shift_ref[...]
    h_ref[...] = h
    lg_ref[...] = jnp.dot(h.astype(BF16), rw_ref[...].astype(BF16), preferred_element_type=F32)


def post_attention(x, y, mods, rows_per_batch, ln_g, ln_b, router_w, alpha, tm=128):
    m, d = x.shape
    tm = min(tm, m)
    n_exp = router_w.shape[1]
    row = pl.BlockSpec((tm, d), lambda i: (i, 0))
    vec = pl.BlockSpec((1, d), lambda i: (0, 0))
    return pl.pallas_call(
        functools.partial(_post_attn_body, alpha=alpha),
        out_shape=(jax.ShapeDtypeStruct((m, d), F32), jax.ShapeDtypeStruct((m, d), F32),
                   jax.ShapeDtypeStruct((m, n_exp), F32)),
        grid=(m // tm,),
        in_specs=[row, row] + _mod_specs(mods, (2, 3, 4), tm, rows_per_batch, d) + [
            vec, vec, pl.BlockSpec((d, n_exp), lambda i: (0, 0))],
        out_specs=(row, row, pl.BlockSpec((tm, n_exp), lambda i: (i, 0))),
        compiler_params=_params(("arbitrary",)),
        name="post_attention",
    )(x, y, mods, mods, mods, ln_g.reshape(1, d), ln_b.reshape(1, d), router_w)


def _post_moe_body(s0_ref, s1_ref, x_ref, gate_ref, g_ref, b_ref, *rest, alpha, tm, tok0, has_next):
    if has_next:
        shift_ref, scale_ref, slots_hbm, xo_ref, hn_ref, buf, sem = rest
    else:
        slots_hbm, xo_ref, buf, sem = rest
    i = pl.program_id(0)

    def copies(t, slot, r):
        tok = tok0 + t * tm + r
        return (pltpu.make_async_copy(slots_hbm.at[pl.ds(s0_ref[tok], 1)], buf.at[slot, 0, pl.ds(r, 1)], sem.at[slot]),
                pltpu.make_async_copy(slots_hbm.at[pl.ds(s1_ref[tok], 1)], buf.at[slot, 1, pl.ds(r, 1)], sem.at[slot]))

    def issue_tile(t, slot):
        def issue(r, c):
            a, b = copies(t, slot, r)
            a.start()
            b.start()
            return c

        lax.fori_loop(0, tm, issue, 0)

    @pl.when(i == 0)
    def _():
        issue_tile(0, 0)

    @pl.when(i + 1 < pl.num_programs(0))
    def _():
        issue_tile(i + 1, (i + 1) % 2)

    slot = i % 2

    def wait(r, c):
        a, b = copies(i, slot, r)
        a.wait()
        b.wait()
        return c

    lax.fori_loop(0, tm, wait, 0)
    f = buf[slot, 0] + buf[slot, 1]
    xn = _layer_norm(alpha * x_ref[...] + gate_ref[...] * f, g_ref[...], b_ref[...])
    xo_ref[...] = xn
    if has_next:
        hn_ref[...] = (xn * (1.0 + scale_ref[...]) + shift_ref[...]).astype(hn_ref.dtype)


def post_moe(x, slots, s0, s1, tok0, mods, mods_next, rows_per_batch, ln_g, ln_b, alpha, tm=128):
    m, d = x.shape
    tm = min(tm, m)
    has_next = mods_next is not None
    row = pl.BlockSpec((tm, d), lambda i, *_: (i, 0))
    vec = pl.BlockSpec((1, d), lambda i, *_: (0, 0))
    in_specs = [row] + _mod_specs(mods, (5,), tm, rows_per_batch, d) + [vec, vec]
    args = [x, mods, ln_g.reshape(1, d), ln_b.reshape(1, d)]
    out_shape = [jax.ShapeDtypeStruct((m, d), F32)]
    out_specs = [row]
    if has_next:
        in_specs += _mod_specs(mods_next, (0, 1), tm, rows_per_batch, d)
        args += [mods_next, mods_next]
        out_shape.append(jax.ShapeDtypeStruct((m, d), BF16))
        out_specs.append(row)
    in_specs.append(pl.BlockSpec(memory_space=pl.ANY))
    args.append(slots)
    out = pl.pallas_call(
        functools.partial(_post_moe_body, alpha=alpha, tm=tm, tok0=tok0, has_next=has_next),
        out_shape=tuple(out_shape),
        grid_spec=pltpu.PrefetchScalarGridSpec(
            num_scalar_prefetch=2, grid=(m // tm,), in_specs=in_specs, out_specs=tuple(out_specs),
            scratch_shapes=[pltpu.VMEM((2, 2, tm, d), F32), pltpu.SemaphoreType.DMA((2,))]),
        compiler_params=_params(("arbitrary",)),
        name="post_moe",
    )(s0, s1, *args)
    return out if has_next else (out[0], None)


def _route_body(lg_ref, bias_ref, ids_ref, w_ref, *, per_group):
    s = jax.nn.sigmoid(lg_ref[...])
    bz = s + bias_ref[...]
    n_exp = s.shape[0]
    rows_b = [bz[e:e + 1, :] for e in range(n_exp)]
    rows_s = [s[e:e + 1, :] for e in range(n_exp)]
    best = None
    for g in range(n_exp // per_group):
        members = range(g * per_group, (g + 1) * per_group)
        rank = {}
        for e in members:
            r = jnp.zeros_like(rows_b[e], dtype=jnp.int32)
            for o in members:
                if o == e:
                    continue
                ahead = (rows_b[o] > rows_b[e]) | ((rows_b[o] == rows_b[e]) & (o < e))
                r = r + ahead.astype(jnp.int32)
            rank[e] = r
        zero = jnp.zeros_like(rows_b[0])
        score = sum(jnp.where(rank[e] < 2, rows_b[e], zero) for e in members)
        e0 = sum(jnp.where(rank[e] == 0, e, 0) for e in members)
        e1 = sum(jnp.where(rank[e] == 1, e, 0) for e in members)
        s0 = sum(jnp.where(rank[e] == 0, rows_s[e], zero) for e in members)
        s1 = sum(jnp.where(rank[e] == 1, rows_s[e], zero) for e in members)
        if best is None:
            best = (score, e0, e1, s0, s1)
        else:
            take = score > best[0]
            best = tuple(jnp.where(take, new, old) for new, old in zip((score, e0, e1, s0, s1), best))
    _, e0, e1, s0, s1 = best
    tot = s0 + s1
    ids_ref[0:1, :] = e0
    ids_ref[1:2, :] = e1
    w_ref[0:1, :] = s0 / tot
    w_ref[1:2, :] = s1 / tot


def route(logits_t, router_bias):
    n_exp, n = logits_t.shape
    return pl.pallas_call(
        functools.partial(_route_body, per_group=n_exp // N_EXPERT_GROUPS),
        out_shape=(jax.ShapeDtypeStruct((2, n), jnp.int32), jax.ShapeDtypeStruct((2, n), F32)),
        name="route",
    )(logits_t, router_bias.reshape(n_exp, 1).astype(F32))


def _dispatch_body(tok_ref, nused_ref, src_a, src_b, o_ref, buf, sem, *, tm, n_a):
    i = pl.program_id(0)
    n_used = nused_ref[0]

    def row_copy(src, row, slot, r):
        return pltpu.make_async_copy(src.at[pl.ds(row, 1)], buf.at[slot, pl.ds(r, 1)], sem.at[slot])

    def issue_tile(t, slot):
        def issue(r, c):
            tok = tok_ref[t * tm + r]

            @pl.when(tok < n_a)
            def _():
                row_copy(src_a, tok, slot, r).start()

            @pl.when(tok >= n_a)
            def _():
                row_copy(src_b, tok - n_a, slot, r).start()

            return c

        lax.fori_loop(0, tm, issue, 0)

    @pl.when(i == 0)
    def _():
        issue_tile(0, 0)

    @pl.when(i + 1 < n_used)
    def _():
        issue_tile(i + 1, (i + 1) % 2)

    @pl.when(i < n_used)
    def _():
        slot = i % 2

        def wait(r, c):
            row_copy(src_a, 0, slot, r).wait()
            return c

        lax.fori_loop(0, tm, wait, 0)
        o_ref[...] = buf[slot].astype(o_ref.dtype)

    @pl.when(i >= n_used)
    def _():
        o_ref[...] = jnp.zeros_like(o_ref)


def dispatch(h_a, h_b, slot_tok, n_used, n_slots, tm):
    n_a, d = h_a.shape
    return pl.pallas_call(
        functools.partial(_dispatch_body, tm=tm, n_a=n_a),
        out_shape=jax.ShapeDtypeStruct((n_slots, d), BF16),
        grid_spec=pltpu.PrefetchScalarGridSpec(
            num_scalar_prefetch=2, grid=(n_slots // tm,),
            in_specs=[pl.BlockSpec(memory_space=pl.ANY), pl.BlockSpec(memory_space=pl.ANY)],
            out_specs=pl.BlockSpec((tm, d), lambda i, tok, nu: (i, 0)),
            scratch_shapes=[pltpu.VMEM((2, tm, d), F32), pltpu.SemaphoreType.DMA((2,))]),
        compiler_params=_params(("arbitrary",)),
        name="dispatch",
    )(slot_tok, n_used, h_a, h_b)


def _gate_up_body(te_ref, first_ref, nused_ref, x_ref, wg_ref, wu_ref, sw_ref, o_ref, wg_bf, wu_bf):
    t = pl.program_id(1)

    @pl.when(first_ref[t] == 1)
    def _():
        wg_bf[...] = wg_ref[...].astype(BF16)
        wu_bf[...] = wu_ref[...].astype(BF16)

    @pl.when(t < nused_ref[0])
    def _():
        x = x_ref[...]
        hg = jnp.dot(x, wg_bf[...], preferred_element_type=F32)
        hu = jnp.dot(x, wu_bf[...], preferred_element_type=F32)
        act = hg * jax.nn.sigmoid(hg) * hu * sw_ref[...]
        o_ref[...] = act.astype(o_ref.dtype)

    @pl.when(t >= nused_ref[0])
    def _():
        o_ref[...] = jnp.zeros_like(o_ref)


def expert_gate_up(xs, w_gate, w_up, slot_w, tile_e, tile_first, n_used, tm, tf=512):
    n_slots, d = xs.shape
    f = w_gate.shape[2]
    tf = min(tf, f)
    n_tiles = n_slots // tm

    def tile(t, nu):
        return jnp.minimum(t, nu[0] - 1)

    w_spec = pl.BlockSpec((None, d, tf), lambda j, t, te, fi, nu: (te[t], 0, j))
    return pl.pallas_call(
        _gate_up_body,
        out_shape=jax.ShapeDtypeStruct((n_slots, f), BF16),
        grid_spec=pltpu.PrefetchScalarGridSpec(
            num_scalar_prefetch=3, grid=(f // tf, n_tiles),
            in_specs=[pl.BlockSpec((tm, d), lambda j, t, te, fi, nu: (tile(t, nu), 0)), w_spec, w_spec,
                      pl.BlockSpec((tm, 1), lambda j, t, te, fi, nu: (tile(t, nu), 0))],
            out_specs=pl.BlockSpec((tm, tf), lambda j, t, te, fi, nu: (t, j)),
            scratch_shapes=[pltpu.VMEM((d, tf), BF16), pltpu.VMEM((d, tf), BF16)]),
        compiler_params=_params(("arbitrary", "arbitrary")),
        name="expert_gate_up",
    )(tile_e, tile_first, n_used, xs, w_gate, w_up, slot_w)


def _down_body(te_ref, first_ref, nused_ref, a_ref, w_ref, o_ref, w_bf):
    t = pl.program_id(1)

    @pl.when(first_ref[t] == 1)
    def _():
        w_bf[...] = w_ref[...].astype(BF16)

    @pl.when(t < nused_ref[0])
    def _():
        o_ref[...] = jnp.dot(a_ref[...], w_bf[...], preferred_element_type=F32)

    @pl.when(t >= nused_ref[0])
    def _():
        o_ref[...] = jnp.zeros_like(o_ref)


def expert_down(act, w_down, tile_e, tile_first, n_used, tm, tn=1024):
    n_slots, f = act.shape
    d = w_down.shape[2]
    tn = min(tn, d)
    n_tiles = n_slots // tm

    def tile(t, nu):
        return jnp.minimum(t, nu[0] - 1)

    return pl.pallas_call(
        _down_body,
        out_shape=jax.ShapeDtypeStruct((n_slots, d), F32),
        grid_spec=pltpu.PrefetchScalarGridSpec(
            num_scalar_prefetch=3, grid=(d // tn, n_tiles),
            in_specs=[pl.BlockSpec((tm, f), lambda j, t, te, fi, nu: (tile(t, nu), 0)),
                      pl.BlockSpec((None, f, tn), lambda j, t, te, fi, nu: (te[t], 0, j))],
            out_specs=pl.BlockSpec((tm, tn), lambda j, t, te, fi, nu: (t, j)),
            scratch_shapes=[pltpu.VMEM((f, tn), BF16)]),
        compiler_params=_params(("arbitrary", "arbitrary")),
        name="expert_down",
    )(tile_e, tile_first, n_used, act, w_down)


def moe_plan(ids, wts, n_exp, tm):
    n = ids.shape[1]
    n_pairs = 2 * n
    n_tiles = -(-(n_pairs + n_exp * (tm - 1)) // tm)
    n_slots = n_tiles * tm
    e_flat = ids.reshape(-1)
    onehot = (e_flat[:, None] == jnp.arange(n_exp, dtype=jnp.int32)[None, :]).astype(jnp.int32)
    csum = jnp.cumsum(onehot, axis=0)
    rank = jnp.sum(onehot * csum, axis=1) - 1
    counts = csum[-1]
    padded = ((counts + tm - 1) // tm) * tm
    ends = jnp.cumsum(padded)
    starts = ends - padded
    slot = (jnp.sum(onehot * starts[None, :], axis=1) + rank).astype(jnp.int32)
    tok = jnp.tile(jnp.arange(n, dtype=jnp.int32), 2)
    slot_tok = jnp.zeros((n_slots,), jnp.int32).at[slot].set(tok)
    slot_w = jnp.zeros((n_slots,), F32).at[slot].set(wts.reshape(-1))
    n_used = (ends[-1] // tm).astype(jnp.int32)
    tile_start = jnp.arange(n_tiles, dtype=jnp.int32) * tm
    tile_start = jnp.minimum(tile_start, ends[-1] - 1)
    tile_e = jnp.minimum(jnp.sum((tile_start[:, None] >= ends[None, :]).astype(jnp.int32), axis=1), n_exp - 1)
    tile_first = jnp.concatenate([jnp.ones((1,), jnp.int32), (tile_e[1:] != tile_e[:-1]).astype(jnp.int32)])
    return dict(n_slots=n_slots, slot_tok=slot_tok, slot_w=slot_w.reshape(n_slots, 1), n_used=n_used.reshape(1),
                tile_e=tile_e.astype(jnp.int32), tile_first=tile_first, s0=slot[:n], s1=slot[n:])


def moe_experts(h_a, h_b, logits, router_bias, w_gate, w_up, w_down, tm=256):
    n_exp = w_gate.shape[0]
    ids, wts = route(logits.T, router_bias)
    plan = moe_plan(ids, wts, n_exp, tm)
    xs = dispatch(h_a, h_b, plan["slot_tok"], plan["n_used"], plan["n_slots"], tm)
    act = expert_gate_up(xs, w_gate, w_up, plan["slot_w"], plan["tile_e"], plan["tile_first"], plan["n_used"], tm)
    out = expert_down(act, w_down, plan["tile_e"], plan["tile_first"], plan["n_used"], tm)
    return out, plan["s0"], plan["s1"]


NEG_INF = float("-inf")
_NT = (((1,), (1,)), ((), ()))


def _heads_first(vals, n_kv):
    lead = vals.ndim - 1
    v = jnp.transpose(vals, (lead,) + tuple(range(lead)))
    return v.reshape((n_kv, v.shape[0] // n_kv) + v.shape[1:])


def band_bias(rel_table, n_kv, max_step, dil):
    w = DIL_BAND
    steps = (jnp.arange(w)[:, None] + w) - jnp.arange(2 * w)[None, :]
    vals = rel_table.astype(F32)[rel_bucket(steps * dil)]
    vals = jnp.where(((steps >= 0) & (steps <= max_step))[..., None], vals, NEG_INF)
    v = _heads_first(vals, n_kv)
    return v.reshape(n_kv, v.shape[1] * w, 2 * w)


def moba_bias(rel_table, n_kv, n_blocks):
    blk = MOBA_BLOCK
    dist = (jnp.arange(n_blocks)[:, None, None] * blk + jnp.arange(blk)[None, :, None]
            - jnp.arange(blk)[None, None, :])
    vals = rel_table.astype(F32)[rel_bucket(dist)]
    vals = jnp.where((dist >= 0)[..., None], vals, NEG_INF)
    v = _heads_first(vals, n_kv)
    v = jnp.transpose(v, (0, 2, 1, 3, 4))
    return v.reshape(n_kv, n_blocks, v.shape[2] * blk, blk)


def _stack_heads(q_ref, r0, rows, g_heads):
    return jnp.concatenate([q_ref[pl.ds(r0, rows), g * HEAD_DIM:(g + 1) * HEAD_DIM] for g in range(g_heads)], axis=0)


def _band_body(*refs, n_blocks, g_heads, with_sink):
    if with_sink:
        q_ref, k_ref, v_ref, bias_ref, sink_ref, o_ref, kb, vb = refs
        lse_ref = None
    else:
        q_ref, k_ref, v_ref, bias_ref, o_ref, lse_ref, kb, vb = refs
    w = DIL_BAND
    kb[...] = k_ref[...].astype(BF16)
    vb[...] = v_ref[...].astype(BF16)

    def block(n, first):
        r0 = 0 if first else pl.multiple_of(n * w, w)
        qs = _stack_heads(q_ref, r0, w, g_heads)
        if first:
            keys, vals, bias = kb[0:w, :], vb[0:w, :], bias_ref[:, w:]
        else:
            keys, vals, bias = kb[pl.ds(r0 - w, 2 * w), :], vb[pl.ds(r0 - w, 2 * w), :], bias_ref[...]
        s = lax.dot_general(qs, keys, _NT, preferred_element_type=F32) + bias
        m = jnp.max(s, axis=1, keepdims=True)
        if with_sink:
            sink = sink_ref[...]
            m = jnp.maximum(m, sink)
            p = jnp.exp(s - m)
            den = jnp.sum(p, axis=1, keepdims=True) + jnp.exp(sink - m)
            p = p * (1.0 / den)
        else:
            lse = jnp.log(jnp.sum(jnp.exp(s - m), axis=1, keepdims=True)) + m
            p = jnp.exp(s - lse)
        out = jnp.dot(p.astype(BF16), vals, preferred_element_type=F32)
        for g in range(g_heads):
            cols = slice(g * HEAD_DIM, (g + 1) * HEAD_DIM)
            o_ref[pl.ds(r0, w), cols] = out[g * w:(g + 1) * w].astype(o_ref.dtype)
            if lse_ref is not None:
                lse_ref[pl.ds(r0, w), cols] = jnp.broadcast_to(lse[g * w:(g + 1) * w], (w, HEAD_DIM))

    block(0, True)
    if n_blocks > 1:
        def body(n, c):
            block(n, False)
            return c
        lax.fori_loop(1, n_blocks, body, 0)


def band_attention(q2, kv2, bias, sinks, *, batch, dil, n_kv, k_col0, v_col0, kv_cols, out_dtype):
    n_tok, qd = q2.shape
    seq = n_tok // batch
    m = seq // dil
    g_heads = qd // HEAD_DIM // n_kv
    gw = g_heads * HEAD_DIM
    assert m % DIL_BAND == 0
    qv = q2.reshape(batch * m, dil * qd)
    kvv = kv2.reshape(batch * m, dil * kv_cols)
    q_per = qd // gw
    kv_per = kv_cols // HEAD_DIM
    kc, vc = k_col0 // HEAD_DIM, v_col0 // HEAD_DIM
    with_sink = sinks is not None
    q_spec = pl.BlockSpec((m, gw), lambda k, r, b: (b, r * q_per + k))
    in_specs = [q_spec,
                pl.BlockSpec((m, HEAD_DIM), lambda k, r, b: (b, r * kv_per + kc + k)),
                pl.BlockSpec((m, HEAD_DIM), lambda k, r, b: (b, r * kv_per + vc + k)),
                pl.BlockSpec((None, gw, 2 * DIL_BAND), lambda k, r, b: (k, 0, 0))]
    args = [qv, kvv, kvv, bias]
    out_shape = [jax.ShapeDtypeStruct(qv.shape, out_dtype)]
    out_specs = [q_spec]
    if with_sink:
        in_specs.append(pl.BlockSpec((None, gw, 1), lambda k, r, b: (k, 0, 0)))
        args.append(jnp.repeat(sinks.astype(F32), DIL_BAND).reshape(n_kv, gw, 1))
    else:
        out_shape.append(jax.ShapeDtypeStruct(qv.shape, F32))
        out_specs.append(q_spec)
    out = pl.pallas_call(
        functools.partial(_band_body, n_blocks=m // DIL_BAND, g_heads=g_heads, with_sink=with_sink),
        out_shape=tuple(out_shape),
        grid=(n_kv, dil, batch),
        in_specs=in_specs,
        out_specs=tuple(out_specs),
        scratch_shapes=[pltpu.VMEM((m, HEAD_DIM), BF16), pltpu.VMEM((m, HEAD_DIM), BF16)],
        compiler_params=_params(("arbitrary", "arbitrary", "arbitrary")),
        name="band_attention",
    )(*args)
    return tuple(o.reshape(n_tok, qd) for o in out)


def _merge_body(o0, o1, o2, l0, l1, l2, out_ref):
    a, b, c = l0[...], l1[...], l2[...]
    mx = jnp.maximum(jnp.maximum(a, b), c)
    ea, eb, ec = jnp.exp(a - mx), jnp.exp(b - mx), jnp.exp(c - mx)
    inv = 1.0 / (ea + eb + ec)
    out_ref[...] = ((ea * inv) * o0[...] + (eb * inv) * o1[...] + (ec * inv) * o2[...]).astype(out_ref.dtype)


def merge_groups(outs, lses, tm=128):
    m, d = outs[0].shape
    tm = min(tm, m)
    row = pl.BlockSpec((tm, d), lambda i: (i, 0))
    return pl.pallas_call(
        _merge_body,
        out_shape=jax.ShapeDtypeStruct((m, d), BF16),
        grid=(m // tm,),
        in_specs=[row] * 6,
        out_specs=row,
        compiler_params=_params(("arbitrary",)),
        name="merge_groups",
    )(*outs, *lses)


def _moba_body(q_ref, k_ref, v_ref, bias_ref, o_ref, kb, vb, ls, *, n_blocks, g_heads):
    blk = MOBA_BLOCK
    rows = g_heads * blk
    kf = k_ref[...]
    kb[...] = kf.astype(BF16)
    vb[...] = v_ref[...].astype(BF16)
    kmean = jnp.concatenate(
        [jnp.sum(kf[n * blk:(n + 1) * blk], axis=0, keepdims=True) * (1.0 / blk) for n in range(n_blocks)],
        axis=0).astype(BF16)
    lane = lax.broadcasted_iota(jnp.int32, (blk, n_blocks), 1)
    lane_s = lax.broadcasted_iota(jnp.int32, (rows, n_blocks), 1)

    def own_block(ob, carry):
        r0 = pl.multiple_of(ob * blk, blk)
        qs = _stack_heads(q_ref, r0, blk, g_heads)
        gate_h = lax.dot_general(qs, kmean, _NT, preferred_element_type=F32)
        gate = gate_h[0:blk]
        for g in range(1, g_heads):
            gate = gate + gate_h[g * blk:(g + 1) * blk]
        gate = jnp.where(lane < ob, gate, NEG_INF)
        rank = jnp.zeros((blk, n_blocks), jnp.int32)
        for m_ in range(n_blocks):
            col = gate[:, m_:m_ + 1]
            rank = rank + ((col > gate) | ((col == gate) & (lane > m_))).astype(jnp.int32)
        sel = ((lane < ob) & (rank < MOBA_TOPK)).astype(F32)
        sel = jnp.concatenate([sel] * g_heads, axis=0)

        def logits(n, mrow):
            keys = kb[pl.ds(pl.multiple_of(n * blk, blk), blk), :]
            s = lax.dot_general(qs, keys, _NT, preferred_element_type=F32) + bias_ref[ob - n]
            chosen = jnp.sum(jnp.where(lane_s == n, sel, 0.0), axis=1, keepdims=True)
            s = jnp.where((chosen > 0.0) | (n == ob), s, NEG_INF)
            ls[n] = s
            return jnp.maximum(mrow, jnp.maximum(s[:, :HEAD_DIM], s[:, HEAD_DIM:]))

        mrow = lax.fori_loop(0, ob + 1, logits, jnp.full((rows, HEAD_DIM), NEG_INF, F32))
        m = jnp.max(mrow, axis=1, keepdims=True)

        def expo(n, lrow):
            e = jnp.exp(ls[n] - m)
            ls[n] = e
            return lrow + (e[:, :HEAD_DIM] + e[:, HEAD_DIM:])

        lrow = lax.fori_loop(0, ob + 1, expo, jnp.zeros((rows, HEAD_DIM), F32))
        inv = 1.0 / jnp.sum(lrow, axis=1, keepdims=True)

        def weigh(n, acc):
            p = (ls[n] * inv).astype(BF16)
            vals = vb[pl.ds(pl.multiple_of(n * blk, blk), blk), :]
            return acc + jnp.dot(p, vals, preferred_element_type=F32)

        acc = lax.fori_loop(0, ob + 1, weigh, jnp.zeros((rows, HEAD_DIM), F32))
        for g in range(g_heads):
            o_ref[pl.ds(r0, blk), g * HEAD_DIM:(g + 1) * HEAD_DIM] = acc[g * blk:(g + 1) * blk].astype(o_ref.dtype)
        return carry

    lax.fori_loop(0, n_blocks, own_block, 0)


def moba_prompt(q2, kv2, bias, *, batch, n_kv):
    n_tok, qd = q2.shape
    seq = n_tok // batch
    n_blocks = seq // MOBA_BLOCK
    g_heads = qd // HEAD_DIM // n_kv
    gw = g_heads * HEAD_DIM
    q_spec = pl.BlockSpec((seq, gw), lambda k, b: (b, k))
    return pl.pallas_call(
        functools.partial(_moba_body, n_blocks=n_blocks, g_heads=g_heads),
        out_shape=jax.ShapeDtypeStruct((n_tok, qd), BF16),
        grid=(n_kv, batch),
        in_specs=[q_spec,
                  pl.BlockSpec((seq, HEAD_DIM), lambda k, b: (b, k)),
                  pl.BlockSpec((seq, HEAD_DIM), lambda k, b: (b, n_kv + k)),
                  pl.BlockSpec((None, n_blocks, g_heads * MOBA_BLOCK, MOBA_BLOCK), lambda k, b: (k, 0, 0, 0))],
        out_specs=q_spec,
        scratch_shapes=[pltpu.VMEM((seq, HEAD_DIM), BF16), pltpu.VMEM((seq, HEAD_DIM), BF16),
                        pltpu.VMEM((n_blocks, g_heads * MOBA_BLOCK, MOBA_BLOCK), F32)],
        compiler_params=_params(("arbitrary", "arbitrary")),
        name="moba_prompt",
    )(q2, kv2, kv2, bias)


def _rows_of_kv_head(n_heads, n_kv, k):
    g = n_heads // n_kv
    row = lax.broadcasted_iota(jnp.int32, (n_heads, 1), 0)
    return (row >= k * g) & (row < (k + 1) * g)


def _per_kv_head_dot(lhs, rhs_of, n_kv, nt):
    n_heads = lhs.shape[0]
    out = None
    for k in range(n_kv):
        rhs = rhs_of(k)
        if nt:
            r = lax.dot_general(lhs, rhs, _NT, preferred_element_type=F32)
        else:
            r = jnp.dot(lhs, rhs, preferred_element_type=F32)
        out = r if out is None else jnp.where(_rows_of_kv_head(n_heads, n_kv, k), r, out)
    return out


def _swa_sample_body(q_ref, kv_ref, bias_ref, sink_ref, o_ref, *, n_kv):
    q = q_ref[...]
    s = _per_kv_head_dot(q, lambda k: kv_ref[:, k * HEAD_DIM:(k + 1) * HEAD_DIM].astype(BF16), n_kv, True)
    s = s + bias_ref[...]
    sink = sink_ref[...]
    m = jnp.maximum(jnp.max(s, axis=1, keepdims=True), sink)
    p = jnp.exp(s - m)
    den = jnp.sum(p, axis=1, keepdims=True) + jnp.exp(sink - m)
    p = (p * (1.0 / den)).astype(BF16)
    o = _per_kv_head_dot(p, lambda k: kv_ref[:, (n_kv + k) * HEAD_DIM:(n_kv + k + 1) * HEAD_DIM].astype(BF16),
                         n_kv, False)
    o_ref[...] = o.astype(o_ref.dtype)


def swa_sample_attention(q3, window_kv, rel_table, sinks, n_kv):
    bsz, n_heads, _ = q3.shape
    w = window_kv.shape[1]
    bias = rel_table.astype(F32)[rel_bucket(w - 1 - jnp.arange(w))].T
    return pl.pallas_call(
        functools.partial(_swa_sample_body, n_kv=n_kv),
        out_shape=jax.ShapeDtypeStruct(q3.shape, BF16),
        grid=(bsz,),
        in_specs=[pl.BlockSpec((None, n_heads, HEAD_DIM), lambda b: (b, 0, 0)),
                  pl.BlockSpec((None, w, window_kv.shape[2]), lambda b: (b, 0, 0)),
                  pl.BlockSpec((n_heads, w), lambda b: (0, 0)),
                  pl.BlockSpec((n_heads, 1), lambda b: (0, 0))],
        out_specs=pl.BlockSpec((None, n_heads, HEAD_DIM), lambda b: (b, 0, 0)),
        compiler_params=_params(("arbitrary",)),
        name="swa_sample",
    )(q3, window_kv, bias, sinks.astype(F32).reshape(n_heads, 1))


def _with_new_row(old, new_row):
    pad = jnp.where(lax.broadcasted_iota(jnp.int32, (8, HEAD_DIM), 0) == 0,
                    jnp.broadcast_to(new_row, (8, HEAD_DIM)), 0.0)
    return jnp.concatenate([old, pad], axis=0).astype(BF16)


def _dil_sample_body(q_ref, b0_ref, b1_ref, b2_ref, new_ref, bias_ref, o_ref, *, n_kv):
    q = q_ref[...]
    gcols = 2 * n_kv * HEAD_DIM
    outs, lses = [], []
    for gi, buf in enumerate((b0_ref, b1_ref, b2_ref)):
        def keys(k, buf=buf, gi=gi):
            c = k * HEAD_DIM
            return _with_new_row(buf[:, c:c + HEAD_DIM], new_ref[:, gi * gcols + c:gi * gcols + c + HEAD_DIM])

        def vals(k, buf=buf, gi=gi):
            c = (n_kv + k) * HEAD_DIM
            return _with_new_row(buf[:, c:c + HEAD_DIM], new_ref[:, gi * gcols + c:gi * gcols + c + HEAD_DIM])

        s = _per_kv_head_dot(q, keys, n_kv, True) + bias_ref[gi]
        m = jnp.max(s, axis=1, keepdims=True)
        lse = jnp.log(jnp.sum(jnp.exp(s - m), axis=1, keepdims=True)) + m
        p = jnp.exp(s - lse).astype(BF16)
        outs.append(_per_kv_head_dot(p, vals, n_kv, False))
        lses.append(lse)
    mx = jnp.maximum(jnp.maximum(lses[0], lses[1]), lses[2])
    e = [jnp.exp(l - mx) for l in lses]
    inv = 1.0 / (e[0] + e[1] + e[2])
    o_ref[...] = ((e[0] * inv) * outs[0] + (e[1] * inv) * outs[1] + (e[2] * inv) * outs[2]).astype(o_ref.dtype)


def dilated_sample_attention(q3, bufs, new_kv, rel_table, n_kv):
    bsz, n_heads, _ = q3.shape
    gcols = 2 * n_kv * HEAD_DIM
    steps = jnp.arange(DIL_BAND, 0, -1)
    bias = []
    for _, dil in DIL_PAIRS:
        old = rel_table.astype(F32)[rel_bucket(dil * steps)].T
        new = rel_table.astype(F32)[rel_bucket(jnp.zeros((1,), jnp.int32))].T
        bias.append(jnp.concatenate([old, new, jnp.full((n_heads, 7), NEG_INF, F32)], axis=1))
    bias = jnp.stack(bias)
    views = []
    for (window, dil), buf in zip(DIL_PAIRS, bufs):
        assert buf.shape[1] == window == DIL_BAND * dil
        views.append(buf.reshape(bsz, DIL_BAND, dil * gcols))
    buf_spec = pl.BlockSpec((None, DIL_BAND, gcols), lambda b: (b, 0, 0))
    return pl.pallas_call(
        functools.partial(_dil_sample_body, n_kv=n_kv),
        out_shape=jax.ShapeDtypeStruct(q3.shape, BF16),
        grid=(bsz,),
        in_specs=[pl.BlockSpec((None, n_heads, HEAD_DIM), lambda b: (b, 0, 0)), buf_spec, buf_spec, buf_spec,
                  pl.BlockSpec((None, 1, new_kv.shape[1]), lambda b: (b, 0, 0)),
                  pl.BlockSpec(bias.shape, lambda b: (0, 0, 0))],
        out_specs=pl.BlockSpec((None, n_heads, HEAD_DIM), lambda b: (b, 0, 0)),
        compiler_params=_params(("arbitrary",)),
        name="dilated_sample",
    )(q3, *views, new_kv.reshape(bsz, 1, -1), bias)


PAGES_PER_STEP = 4


def _kmean_body(pt_ref, *refs, pages_per_block):
    page_refs, o_ref = refs[:-1], refs[-1]
    j = pl.program_id(1)
    blocks_per_step = len(page_refs) // pages_per_block
    for i in range(blocks_per_step):
        tot = None
        for p in page_refs[i * pages_per_block:(i + 1) * pages_per_block]:
            part = jnp.sum(p[...], axis=0, keepdims=True)
            tot = part if tot is None else tot + part
        o_ref[pl.ds(j * blocks_per_step + i, 1), :] = tot * (1.0 / MOBA_BLOCK)


def paged_block_means(pages, slot, page_table, k_cols):
    n_pool, page = pages.shape[1], pages.shape[2]
    bsz, n_pages = page_table.shape
    pages_per_block = MOBA_BLOCK // page
    assert n_pages % PAGES_PER_STEP == 0 and PAGES_PER_STEP % pages_per_block == 0
    n_blocks = n_pages // pages_per_block
    specs = [pl.BlockSpec((None, None, page, k_cols), functools.partial(
        lambda b, j, pt, i: (slot, pt[b * n_pages + j * PAGES_PER_STEP + i], 0, 0), i=i)) for i in range(PAGES_PER_STEP)]
    return pl.pallas_call(
        functools.partial(_kmean_body, pages_per_block=pages_per_block),
        out_shape=jax.ShapeDtypeStruct((bsz, n_blocks, k_cols), F32),
        grid_spec=pltpu.PrefetchScalarGridSpec(
            num_scalar_prefetch=1, grid=(bsz, n_pages // PAGES_PER_STEP), in_specs=specs,
            out_specs=pl.BlockSpec((None, n_blocks, k_cols), lambda b, j, pt: (b, 0, 0))),
        compiler_params=_params(("arbitrary", "arbitrary")),
        name="paged_block_means",
    )(page_table.reshape(-1), *([pages] * PAGES_PER_STEP))


def _moba_select_body(q_ref, km_ref, sel_ref, *, n_kv):
    q = q_ref[...]
    n_heads = q.shape[0]
    n_blocks = km_ref.shape[0]
    rows = []
    for k in range(n_kv):
        km = km_ref[:, k * HEAD_DIM:(k + 1) * HEAD_DIM].astype(BF16)
        g_all = lax.dot_general(q, km, _NT, preferred_element_type=F32)
        rows.append(jnp.sum(jnp.where(_rows_of_kv_head(n_heads, n_kv, k), g_all, 0.0), axis=0, keepdims=True))
    gate = jnp.concatenate(rows, axis=0)
    lane = lax.broadcasted_iota(jnp.int32, gate.shape, 1)
    rank = jnp.zeros(gate.shape, jnp.int32)
    for m_ in range(n_blocks):
        col = gate[:, m_:m_ + 1]
        rank = rank + ((col > gate) | ((col == gate) & (lane > m_))).astype(jnp.int32)
    out_lane = lax.broadcasted_iota(jnp.int32, sel_ref.shape, 1)
    out = jnp.zeros(sel_ref.shape, jnp.int32)
    for r in range(MOBA_TOPK):
        idx = jnp.sum(jnp.where(rank == r, lane, 0), axis=1, keepdims=True)
        out = jnp.where(out_lane == r, idx, out)
    sel_ref[...] = out


def moba_select(q3, kmean, n_kv):
    bsz, n_heads, _ = q3.shape
    n_blocks = kmean.shape[1]
    assert n_blocks >= MOBA_TOPK
    sel = pl.pallas_call(
        functools.partial(_moba_select_body, n_kv=n_kv),
        out_shape=jax.ShapeDtypeStruct((bsz, n_kv, HEAD_DIM), jnp.int32),
        grid=(bsz,),
        in_specs=[pl.BlockSpec((None, n_heads, HEAD_DIM), lambda b: (b, 0, 0)),
                  pl.BlockSpec((None, n_blocks, kmean.shape[2]), lambda b: (b, 0, 0))],
        out_specs=pl.BlockSpec((None, n_kv, HEAD_DIM), lambda b: (b, 0, 0)),
        compiler_params=_params(("arbitrary",)),
        name="moba_select",
    )(q3, kmean)
    return sel[:, :, :MOBA_TOPK]


def _moba_sample_body(pt_ref, sel_ref, q_ref, k0_ref, k1_ref, v0_ref, v1_ref, pb_ref, kn_ref, vn_ref, b0_ref,
                      o_ref, s_sc, v_sc):
    j = pl.program_id(2)
    q = q_ref[...]
    keys = jnp.concatenate([k0_ref[...], k1_ref[...]], axis=0).astype(BF16)
    s_sc[j] = lax.dot_general(q, keys, _NT, preferred_element_type=F32) + pb_ref[...]
    v_sc[j] = jnp.concatenate([v0_ref[...], v1_ref[...]], axis=0).astype(BF16)

    @pl.when(j == MOBA_TOPK - 1)
    def _():
        kn = kn_ref[...].astype(BF16).astype(F32)
        vn = vn_ref[...].astype(BF16).astype(F32)
        s_new = jnp.sum(q.astype(F32) * kn, axis=1, keepdims=True) + b0_ref[...]
        m = s_new
        for t in range(MOBA_TOPK):
            m = jnp.maximum(m, jnp.max(s_sc[t], axis=1, keepdims=True))
        e_new = jnp.exp(s_new - m)
        es = [jnp.exp(s_sc[t] - m) for t in range(MOBA_TOPK)]
        tot = e_new
        for e in es:
            tot = tot + jnp.sum(e, axis=1, keepdims=True)
        inv = 1.0 / tot
        out = (e_new * inv).astype(BF16).astype(F32) * vn
        sel_out = None
        for t in range(MOBA_TOPK):
            part = jnp.dot((es[t] * inv).astype(BF16), v_sc[t], preferred_element_type=F32)
            sel_out = part if sel_out is None else sel_out + part
        o_ref[...] = (sel_out + out).astype(o_ref.dtype)


def moba_sample_attention(q3, pages, slot, page_table, sel, new_kv, rel_table, n_kv):
    bsz, n_heads, _ = q3.shape
    g = n_heads // n_kv
    page = pages.shape[2]
    n_pages = page_table.shape[1]
    assert MOBA_BLOCK == 2 * page
    past_len = n_pages * page
    n_blocks = past_len // MOBA_BLOCK
    pos = jnp.arange(past_len, dtype=jnp.int32)
    pb = rel_table.astype(F32)[rel_bucket(past_len - pos)]
    pb = jnp.transpose(pb.reshape(n_blocks, MOBA_BLOCK, n_kv, g), (2, 0, 3, 1))
    b0 = rel_table.astype(F32)[rel_bucket(jnp.zeros((), jnp.int32))].reshape(n_kv, g, 1)

    def page_spec(col0, half):
        def index(b, k, j, pt, sl):
            blk = sl[(b * n_kv + k) * MOBA_TOPK + j]
            return (slot, pt[b * n_pages + 2 * blk + half], 0, col0 + k)
        return pl.BlockSpec((None, None, page, HEAD_DIM), index)

    head = pl.BlockSpec((None, None, g, HEAD_DIM), lambda b, k, j, pt, sl: (b, k, 0, 0))
    out = pl.pallas_call(
        _moba_sample_body,
        out_shape=jax.ShapeDtypeStruct((bsz, n_kv, g, HEAD_DIM), BF16),
        grid_spec=pltpu.PrefetchScalarGridSpec(
            num_scalar_prefetch=2, grid=(bsz, n_kv, MOBA_TOPK),
            in_specs=[head, page_spec(0, 0), page_spec(0, 1), page_spec(n_kv, 0), page_spec(n_kv, 1),
                      pl.BlockSpec((None, None, g, MOBA_BLOCK),
                                   lambda b, k, j, pt, sl: (k, sl[(b * n_kv + k) * MOBA_TOPK + j], 0, 0)),
                      pl.BlockSpec((None, None, 1, HEAD_DIM), lambda b, k, j, pt, sl: (b, k, 0, 0)),
                      pl.BlockSpec((None, None, 1, HEAD_DIM), lambda b, k, j, pt, sl: (b, n_kv + k, 0, 0)),
                      pl.BlockSpec((None, g, 1), lambda b, k, j, pt, sl: (k, 0, 0))],
            out_specs=head,
            scratch_shapes=[pltpu.VMEM((MOBA_TOPK, g, MOBA_BLOCK), F32),
                            pltpu.VMEM((MOBA_TOPK, MOBA_BLOCK, HEAD_DIM), BF16)]),
        compiler_params=_params(("arbitrary", "arbitrary", "arbitrary")),
        name="moba_sample",
    )(page_table.reshape(-1), sel.reshape(-1), q3.reshape(bsz, n_kv, g, HEAD_DIM), pages, pages, pages, pages, pb,
      new_kv.reshape(bsz, 2 * n_kv, 1, HEAD_DIM), new_kv.reshape(bsz, 2 * n_kv, 1, HEAD_DIM), b0)
    return out.reshape(bsz, n_heads, HEAD_DIM)


def rel_bucket(dist):
    n = jnp.maximum(dist, 0)
    exact = REL_BUCKETS // 2
    scaled = (jnp.log(jnp.maximum(n, exact).astype(F32) / exact)
              / math.log(REL_MAX_DISTANCE / exact) * (REL_BUCKETS - exact))
    large = jnp.minimum(exact + scaled.astype(jnp.int32), REL_BUCKETS - 1)
    return jnp.where(n < exact, n, large)


def rel_bias(dist, rel_table):
    return rel_table.astype(F32)[rel_bucket(dist)]


def sink_softmax(logits, sink):
    m = jnp.maximum(jnp.max(logits, axis=-1, keepdims=True), sink)
    p = jnp.exp(logits - m)
    return p / (jnp.sum(p, axis=-1, keepdims=True) + jnp.exp(sink - m))


def swa_prompt(q, kv, sinks, rel_table):
    b, s = q.shape[:2]
    n_heads = q.shape[2]
    kvh = kv.shape[3]
    gqa = n_heads // kvh
    w = SWA_WINDOW
    nb = s // w
    qb = q.reshape(b, nb, w, kvh, gqa, HEAD_DIM)
    kvb = kv.reshape(b, nb, w, 2, kvh, HEAD_DIM)
    prev = jnp.pad(kvb, ((0, 0), (1, 0), (0, 0), (0, 0), (0, 0), (0, 0)))[:, :-1]
    band = jnp.concatenate([prev, kvb], axis=2)
    logits = jnp.einsum('bnqkgd,bnskd->bnkgqs', qb, band[:, :, :, 0]).astype(F32)
    dist = (jnp.arange(w)[:, None] + w) - jnp.arange(2 * w)[None, :]
    key_pos = jnp.arange(nb)[:, None, None] * w - w + jnp.arange(2 * w)[None, None, :]
    keep = (dist >= 0) & (dist < w) & (key_pos >= 0)
    bias = jnp.transpose(rel_bias(dist, rel_table), (2, 0, 1)).reshape(kvh, gqa, w, 2 * w)
    logits = jnp.where(keep[None, :, None, None], logits + bias, -jnp.inf)
    p = sink_softmax(logits, sinks.astype(F32).reshape(kvh, gqa, 1, 1))
    out = jnp.einsum('bnkgqs,bnskd->bnqkgd', p.astype(kv.dtype), band[:, :, :, 1])
    return out.reshape(b, s, n_heads, HEAD_DIM)


def swa_sample(q, kv_new, buf, sinks, rel_table):
    b, t = q.shape[:2]
    n_heads = q.shape[2]
    kvh = buf.shape[3]
    gqa = n_heads // kvh
    lb = buf.shape[1]
    kk = jnp.concatenate([buf, kv_new], axis=1)
    dist = (lb + jnp.arange(t))[:, None] - jnp.arange(lb + t)[None, :]
    keep = (dist >= 0) & (dist < SWA_WINDOW)
    qg = q.reshape(b, t, kvh, gqa, HEAD_DIM)
    logits = jnp.einsum('btkgd,bskd->bkgts', qg, kk[:, :, 0]).astype(F32)
    bias = jnp.transpose(rel_bias(dist, rel_table), (2, 0, 1)).reshape(kvh, gqa, t, lb + t)
    logits = jnp.where(keep, logits + bias, -jnp.inf)
    p = sink_softmax(logits, sinks.astype(F32).reshape(kvh, gqa, 1, 1))
    out = jnp.einsum('bkgts,bskd->btkgd', p.astype(kk.dtype), kk[:, :, 1]).reshape(b, t, n_heads, HEAD_DIM)
    new_len = min(SWA_WINDOW, lb + t)
    return out, kk[:, lb + t - new_len:]


def moba_attention(q, kv, q_pos, rel_table, q_chunk):
    b, sq = q.shape[:2]
    n_heads = q.shape[2]
    kvh = kv.shape[3]
    gqa = n_heads // kvh
    L = kv.shape[1]
    nb = -(-L // MOBA_BLOCK)
    kvb = jnp.pad(kv, ((0, 0), (0, nb * MOBA_BLOCK - L), (0, 0), (0, 0), (0, 0)))
    kvb = kvb.reshape(b, nb, MOBA_BLOCK, 2, kvh, HEAD_DIM)
    kmean = jnp.mean(kvb[:, :, :, 0], axis=2, dtype=F32)
    n_sel = min(MOBA_TOPK, nb)
    tab = rel_table.astype(F32).T.reshape(kvh, gqa, REL_BUCKETS)
    b_i = jnp.arange(b)[:, None, None, None]
    kv_i = jnp.arange(kvh)[None, None, :, None]
    t_i = jnp.arange(MOBA_BLOCK)

    def chunk(args):
        qc, pc = args
        qc = qc.reshape(b, q_chunk, kvh, gqa, HEAD_DIM)
        own = pc[0] // MOBA_BLOCK
        gate = jnp.einsum('bqkgd,bnkd->bqkn', qc.astype(F32), kmean)
        gate = jnp.where(jnp.arange(nb) < own, gate, -jnp.inf)
        _, blk = lax.top_k(gate, n_sel)
        sel = kvb[b_i, blk, :, :, kv_i]
        s_log = jnp.einsum('bqkgd,bqkntd->bqkgnt', qc, sel[..., 0, :]).astype(F32)
        s_dist = pc[None, :, None, None, None] - (blk[..., None] * MOBA_BLOCK + t_i)
        s_bias = tab[jnp.arange(kvh)[:, None, None, None], jnp.arange(gqa)[:, None, None],
                     rel_bucket(s_dist)[:, :, :, None]]
        s_keep = (blk < own)[:, :, :, None, :, None]
        s_log = jnp.where(s_keep, s_log + s_bias, -jnp.inf)
        own_kv = lax.dynamic_index_in_dim(kvb, own, axis=1, keepdims=False)
        o_log = jnp.einsum('bqkgd,btkd->bqkgt', qc, own_kv[:, :, 0]).astype(F32)
        o_dist = pc[:, None] - (own * MOBA_BLOCK + t_i)[None, :]
        o_bias = jnp.moveaxis(rel_bias(o_dist, rel_table), -1, 1).reshape(q_chunk, kvh, gqa, MOBA_BLOCK)
        o_log = jnp.where(o_dist[:, None, None, :] >= 0, o_log + o_bias, -jnp.inf)
        logits = jnp.concatenate([s_log.reshape(b, q_chunk, kvh, gqa, n_sel * MOBA_BLOCK), o_log], axis=-1)
        p = jax.nn.softmax(logits, axis=-1).astype(kv.dtype)
        p_sel = p[..., :n_sel * MOBA_BLOCK].reshape(b, q_chunk, kvh, gqa, n_sel, MOBA_BLOCK)
        out = (jnp.einsum('bqkgnt,bqkntd->bqkgd', p_sel, sel[..., 1, :])
               + jnp.einsum('bqkgt,btkd->bqkgd', p[..., n_sel * MOBA_BLOCK:], own_kv[:, :, 1]))
        return out.reshape(b, q_chunk, n_heads, HEAD_DIM)

    nc = sq // q_chunk
    qs = jnp.moveaxis(q.reshape(b, nc, q_chunk, n_heads, HEAD_DIM), 1, 0)
    ps = q_pos.reshape(nc, q_chunk)
    out = lax.map(chunk, (qs, ps))
    return jnp.moveaxis(out, 0, 1).reshape(b, sq, n_heads, HEAD_DIM)


def dilated_prompt(q, kv, dil, rel_table):
    b, s = q.shape[:2]
    n_heads = q.shape[2]
    kvh = kv.shape[3]
    gqa = n_heads // kvh
    m = s // dil
    mb = -(-m // DIL_BAND)
    mp = mb * DIL_BAND

    def to_sub(x):
        x = jnp.moveaxis(x.reshape((b, m, dil) + x.shape[2:]), 2, 1)
        x = jnp.pad(x, [(0, 0), (0, 0), (0, mp - m)] + [(0, 0)] * (x.ndim - 3))
        return x.reshape((b, dil, mb, DIL_BAND) + x.shape[3:])

    qs = to_sub(q.reshape(b, s, kvh, gqa, HEAD_DIM))
    kvs = to_sub(kv)
    prev = jnp.pad(kvs, ((0, 0), (0, 0), (1, 0), (0, 0), (0, 0), (0, 0), (0, 0)))[:, :, :-1]
    band = jnp.concatenate([prev, kvs], axis=3)
    logits = jnp.einsum('brnqkgd,brnskd->brnkgqs', qs, band[:, :, :, :, 0]).astype(F32)
    steps = (jnp.arange(DIL_BAND)[:, None] + DIL_BAND) - jnp.arange(2 * DIL_BAND)[None, :]
    sub_pos = jnp.arange(mb)[:, None, None] * DIL_BAND - DIL_BAND + jnp.arange(2 * DIL_BAND)[None, None, :]
    keep = (steps >= 0) & (steps <= DIL_BAND) & (sub_pos >= 0)
    bias = jnp.transpose(rel_bias(steps * dil, rel_table), (2, 0, 1)).reshape(kvh, gqa, DIL_BAND, 2 * DIL_BAND)
    logits = jnp.where(keep[None, None, :, None, None], logits + bias, -jnp.inf)
    lse = jax.nn.logsumexp(logits, axis=-1)
    p = jnp.exp(logits - lse[..., None]).astype(kv.dtype)
    out = jnp.einsum('brnkgqs,brnskd->brnqkgd', p, band[:, :, :, :, 1])
    out = jnp.moveaxis(out.reshape(b, dil, mp, n_heads, HEAD_DIM)[:, :, :m], 1, 2).reshape(b, s, n_heads, HEAD_DIM)
    lse = jnp.moveaxis(lse, -1, 3).reshape(b, dil, mp, n_heads)[:, :, :m]
    lse = jnp.moveaxis(lse, 1, 2).reshape(b, s, n_heads)
    return out, lse


def dilated_sample(q, kv_new, buf, dil, window, rel_table):
    b, t = q.shape[:2]
    n_heads = q.shape[2]
    kvh = buf.shape[3]
    gqa = n_heads // kvh
    lb = buf.shape[1]
    kk = jnp.concatenate([buf, kv_new], axis=1)
    k_steps = jnp.arange(DIL_BAND + 1)
    idx = lb + jnp.arange(t)[:, None] - dil * k_steps[None, :]
    keep = idx >= 0
    g = kk[:, jnp.maximum(idx, 0)]
    qg = q.reshape(b, t, kvh, gqa, HEAD_DIM)
    logits = jnp.einsum('btkgd,btjkd->btkgj', qg, g[:, :, :, 0]).astype(F32)
    bias = rel_bias(dil * k_steps, rel_table).T.reshape(kvh, gqa, DIL_BAND + 1)
    logits = jnp.where(keep[None, :, None, None, :], logits + bias, -jnp.inf)
    lse = jax.nn.logsumexp(logits, axis=-1)
    p = jnp.exp(logits - lse[..., None]).astype(kk.dtype)
    out = jnp.einsum('btkgj,btjkd->btkgd', p, g[:, :, :, 1]).reshape(b, t, n_heads, HEAD_DIM)
    new_len = min(window, lb + t)
    return out, lse.reshape(b, t, n_heads), kk[:, lb + t - new_len:]


def merge_by_denominator(outs, lses):
    wts = jax.nn.softmax(jnp.stack(lses), axis=0)
    return jnp.sum(wts[..., None] * jnp.stack(outs).astype(F32), axis=0)


def kernel(x_prompt, x_sample, cache_a_kv, cache_b_kv_pages, cache_c_kv_g0, cache_c_kv_g1, cache_c_kv_g2,
           page_table, c_prompt, c_sample, rel_table, ada_w, ada_b, ln_g, ln_b,
           a_w_qkv, a_w_o, a_sinks, b_w_qkv, b_w_o, c_w_qkv, c_w_o,
           router_w, router_bias, moe_w_gate, moe_w_up, moe_w_down):
    bp, seq, d = x_prompt.shape
    bs, dseq, _ = x_sample.shape
    depth = ada_w.shape[0]
    n_heads = d // HEAD_DIM
    qd = n_heads * HEAD_DIM
    kvh = n_heads // GQA
    dil_kvh = kvh // 2
    n_dil = len(DIL_PAIRS)
    alpha = (2 * depth) ** 0.25
    q_scale = HEAD_DIM ** -0.5
    mp, ms = bp * seq, bs * dseq
    past_len = page_table.shape[1] * cache_b_kv_pages.shape[2]

    n_c = bp + bs
    c_rows = -(-n_c // 8) * 8
    c_all = jnp.concatenate([c_prompt, c_sample, jnp.zeros((c_rows - n_c, d), F32)], axis=0)
    mods_all = adaln_all(c_all, ada_w, ada_b)

    def layer_mods(i):
        return mods_all[i, :bp].reshape(bp, 1, 6 * d), mods_all[i, bp:n_c]

    xp = x_prompt.reshape(mp, d)
    xs = x_sample.reshape(ms, d)
    mods_p, mods_s = layer_mods(0)
    hp = modulate(xp, mods_p, seq)
    hs = modulate(xs, mods_s, 1)

    bias_a = band_bias(rel_table, kvh, SWA_WINDOW - 1, 1)
    bias_b = moba_bias(rel_table, kvh, seq // MOBA_BLOCK)
    bias_c = [band_bias(rel_table, dil_kvh, DIL_BAND, dil) for _, dil in DIL_PAIRS]

    a_p, a_s, b_p, b_s = [], [], [], []
    c_p = [[] for _ in range(n_dil)]
    c_s = [[] for _ in range(n_dil)]
    slot_of = [0] * LAYER_KINDS
    for i in range(depth):
        kind = i % LAYER_KINDS
        slot = slot_of[kind]
        slot_of[kind] += 1
        w_qkv, w_o = ((a_w_qkv, a_w_o), (b_w_qkv, b_w_o), (c_w_qkv, c_w_o))[kind]
        w_qkv, w_o = w_qkv[slot], w_o[slot]
        kv_cols = w_qkv.shape[1] - qd
        qp = project(hp, w_qkv, 0, qd, BF16, q_scale)
        kvp2 = project(hp, w_qkv, qd, kv_cols, F32)
        qs = project(hs, w_qkv, 0, qd, BF16, q_scale).reshape(bs, n_heads, HEAD_DIM)
        kvs2 = project(hs, w_qkv, qd, kv_cols, F32)
        if kind == 0:
            (op,) = band_attention(qp, kvp2, bias_a, a_sinks[slot], batch=bp, dil=1, n_kv=kvh, k_col0=0,
                                   v_col0=kvh * HEAD_DIM, kv_cols=kv_cols, out_dtype=BF16)
            kvp = kvp2.reshape(bp, seq, 2, kvh, HEAD_DIM)
            kvs = kvs2.reshape(bs, dseq, 2, kvh, HEAD_DIM)
            assert cache_a_kv.shape[2] == SWA_WINDOW
            new_buf = jnp.concatenate([cache_a_kv[slot][:, 1:], kvs], axis=1)
            os_ = swa_sample_attention(qs, new_buf.reshape(bs, SWA_WINDOW, kv_cols), rel_table, a_sinks[slot], kvh)
            a_p.append(kvp[:, seq - min(SWA_WINDOW, seq):])
            a_s.append(new_buf)
        elif kind == 1:
            op = moba_prompt(qp, kvp2, bias_b, batch=bp, n_kv=kvh)
            pages = cache_b_kv_pages.reshape(cache_b_kv_pages.shape[:3] + (kv_cols,))
            kmean = paged_block_means(pages, slot, page_table, kvh * HEAD_DIM)
            sel = moba_select(qs, kmean, kvh)
            os_ = moba_sample_attention(qs, pages, slot, page_table, sel, kvs2, rel_table, kvh)
            b_p.append(kvp2.reshape(bp, seq, 2, kvh, HEAD_DIM))
            b_s.append(kvs2.reshape(bs, dseq, 2, kvh, HEAD_DIM))
        else:
            kvp = kvp2.reshape(bp, seq, n_dil, 2, dil_kvh, HEAD_DIM)
            kvs = kvs2.reshape(bs, dseq, n_dil, 2, dil_kvh, HEAD_DIM)
            bufs = (cache_c_kv_g0[slot], cache_c_kv_g1[slot], cache_c_kv_g2[slot])
            op_l, lp_l = [], []
            group_cols = 2 * dil_kvh * HEAD_DIM
            for g, (window, dil) in enumerate(DIL_PAIRS):
                o, l = band_attention(qp, kvp2, bias_c[g], None, batch=bp, dil=dil, n_kv=dil_kvh,
                                      k_col0=g * group_cols, v_col0=g * group_cols + dil_kvh * HEAD_DIM,
                                      kv_cols=kv_cols, out_dtype=F32)
                op_l.append(o)
                lp_l.append(l)
                c_s[g].append(jnp.concatenate([bufs[g][:, 1:], kvs[:, :, g]], axis=1))
                c_p[g].append(kvp[:, seq - min(window, seq):, g])
            op = merge_groups(op_l, lp_l)
            os_ = dilated_sample_attention(qs, bufs, kvs2, rel_table, dil_kvh)
        yp = project(op, w_o, 0, d, F32)
        ys = project(os_.reshape(ms, qd), w_o, 0, d, F32)

        xp, h2p, lgp = post_attention(xp, yp, mods_p, seq, ln_g[i, 0], ln_b[i, 0], router_w, alpha)
        xs, h2s, lgs = post_attention(xs, ys, mods_s, 1, ln_g[i, 0], ln_b[i, 0], router_w, alpha)
        slots, s0, s1 = moe_experts(h2p, h2s, jnp.concatenate([lgp, lgs], axis=0),
                                    router_bias, moe_w_gate[i], moe_w_up[i], moe_w_down[i])
        if i + 1 < depth:
            nxt_p, nxt_s = layer_mods(i + 1)
        else:
            nxt_p = nxt_s = None
        xp, hp = post_moe(xp, slots, s0, s1, 0, mods_p, nxt_p, seq, ln_g[i, 1], ln_b[i, 1], alpha)
        xs, hs = post_moe(xs, slots, s0, s1, mp, mods_s, nxt_s, 1, ln_g[i, 1], ln_b[i, 1], alpha)
        mods_p, mods_s = nxt_p, nxt_s

    return (xp.reshape(bp, seq, d), xs.reshape(bs, dseq, d), jnp.stack(a_p), jnp.stack(a_s),
            jnp.stack(b_p), jnp.stack(b_s), jnp.stack(c_p[0]), jnp.stack(c_s[0]), jnp.stack(c_p[1]),
            jnp.stack(c_s[1]), jnp.stack(c_p[2]), jnp.stack(c_s[2]))
```

```python
import functools
import math

import jax
import jax.numpy as jnp
from jax import lax
from jax.experimental import pallas as pl
from jax.experimental.pallas import tpu as pltpu

F32 = jnp.float32
BF16 = jnp.bfloat16
HIGHEST = lax.Precision.HIGHEST

HEAD_DIM = 128
GQA = 4
SWA_WINDOW = 128
MOBA_BLOCK = 256
MOBA_TOPK = 3
MOBA_Q_CHUNK = 8
DIL_PAIRS = ((128, 1), (512, 4), (2048, 16))
DIL_BAND = 128
REL_BUCKETS = 32
REL_MAX_DISTANCE = 2048
N_EXPERT_GROUPS = 4
LN_EPS = 1e-5
LAYER_KINDS = 3

VMEM_LIMIT = 56 * 1024 * 1024


def _params(sem, vmem=VMEM_LIMIT):
    return pltpu.CompilerParams(dimension_semantics=sem, vmem_limit_bytes=vmem)


def _adaln_body(c_ref, w_ref, b_ref, o_ref):
    c = c_ref[...]
    a = c * jax.nn.sigmoid(c)
    o_ref[...] = jnp.dot(a.astype(BF16), w_ref[...].astype(BF16), preferred_element_type=F32) + b_ref[...]


def adaln_all(c, ada_w, ada_b, tn=512):
    n_layers, d, n = ada_w.shape
    tn = min(tn, n)
    r = c.shape[0]
    return pl.pallas_call(
        _adaln_body,
        out_shape=jax.ShapeDtypeStruct((n_layers, r, n), F32),
        grid=(n_layers, n // tn),
        in_specs=[
            pl.BlockSpec((r, d), lambda l, j: (0, 0)),
            pl.BlockSpec((None, d, tn), lambda l, j: (l, 0, j)),
            pl.BlockSpec((None, 1, tn), lambda l, j: (l, 0, j)),
        ],
        out_specs=pl.BlockSpec((None, r, tn), lambda l, j: (l, 0, j)),
        compiler_params=_params(("arbitrary", "arbitrary")),
        name="adaln",
    )(c, ada_w, ada_b.reshape(n_layers, 1, n))


def _proj_body(x_ref, w_ref, o_ref, wbf_ref, *, scale):
    @pl.when(pl.program_id(1) == 0)
    def _():
        wbf_ref[...] = w_ref[...].astype(BF16)

    acc = jnp.dot(x_ref[...], wbf_ref[...], preferred_element_type=F32)
    if scale != 1.0:
        acc = acc * scale
    o_ref[...] = acc.astype(o_ref.dtype)


def project(x, w, lead, col0, ncols, out_dtype, scale=1.0, tn=512):
    m, k = x.shape
    tm = min(m, 512)
    tn = min(tn, ncols)
    assert m % tm == 0 and ncols % tn == 0 and col0 % tn == 0
    j0 = col0 // tn
    return pl.pallas_call(
        functools.partial(_proj_body, scale=scale),
        out_shape=jax.ShapeDtypeStruct((m, ncols), out_dtype),
        grid=(ncols // tn, m // tm),
        in_specs=[
            pl.BlockSpec((tm, k), lambda j, i: (i, 0)),
            pl.BlockSpec((None, k, tn), lambda j, i: (lead, 0, j + j0)),
        ],
        out_specs=pl.BlockSpec((tm, tn), lambda j, i: (i, j)),
        scratch_shapes=[pltpu.VMEM((k, tn), BF16)],
        compiler_params=_params(("arbitrary", "arbitrary")),
        name="project",
    )(x, w)


def _layer_norm(v, g, b):
    mu = jnp.mean(v, axis=-1, keepdims=True)
    d = v - mu
    var = jnp.mean(d * d, axis=-1, keepdims=True)
    return d * lax.rsqrt(var + LN_EPS) * g + b


def _modulate_body(x_ref, shift_ref, scale_ref, h_ref):
    h_ref[...] = (x_ref[...] * (1.0 + scale_ref[...]) + shift_ref[...]).astype(h_ref.dtype)


def _mod_specs(mods, parts, tm, rows_per_batch, d):
    if mods.ndim == 3:
        tiles_per_batch = rows_per_batch // tm
        return [pl.BlockSpec((None, 1, d), functools.partial(
            lambda i, *_, p: (i // tiles_per_batch, 0, p), p=p)) for p in parts]
    return [pl.BlockSpec((tm, d), functools.partial(lambda i, *_, p: (i, p), p=p)) for p in parts]


def modulate(x, mods, rows_per_batch, tm=128):
    m, d = x.shape
    tm = min(tm, m)
    row = pl.BlockSpec((tm, d), lambda i: (i, 0))
    return pl.pallas_call(
        _modulate_body,
        out_shape=jax.ShapeDtypeStruct((m, d), BF16),
        grid=(m // tm,),
        in_specs=[row] + _mod_specs(mods, (0, 1), tm, rows_per_batch, d),
        out_specs=row,
        compiler_params=_params(("arbitrary",)),
        name="modulate",
    )(x, mods, mods)


def _post_attn_body(x_ref, y_ref, gate_ref, shift_ref, scale_ref, g_ref, b_ref, rw_ref,
                    xo_ref, h_ref, lg_ref, *, alpha):
    xm = _layer_norm(alpha * x_ref[...] + gate_ref[...] * y_ref[...], g_ref[...], b_ref[...])
    xo_ref[...] = xm
    h = xm * (1.0 + scale_ref[...]) + shift_ref[...]
    h_ref[...] = h
    lg_ref[...] = jnp.dot(h.astype(BF16), rw_ref[...].astype(BF16), preferred_element_type=F32)


def post_attention(x, y, mods, rows_per_batch, ln_g, ln_b, router_w, alpha, tm=128):
    m, d = x.shape
    tm = min(tm, m)
    n_exp = router_w.shape[1]
    row = pl.BlockSpec((tm, d), lambda i: (i, 0))
    vec = pl.BlockSpec((1, d), lambda i: (0, 0))
    return pl.pallas_call(
        functools.partial(_post_attn_body, alpha=alpha),
        out_shape=(jax.ShapeDtypeStruct((m, d), F32), jax.ShapeDtypeStruct((m, d), F32),
                   jax.ShapeDtypeStruct((m, n_exp), F32)),
        grid=(m // tm,),
        in_specs=[row, row] + _mod_specs(mods, (2, 3, 4), tm, rows_per_batch, d) + [
            vec, vec, pl.BlockSpec((d, n_exp), lambda i: (0, 0))],
        out_specs=(row, row, pl.BlockSpec((tm, n_exp), lambda i: (i, 0))),
        compiler_params=_params(("arbitrary",)),
        name="post_attention",
    )(x, y, mods, mods, mods, ln_g.reshape(1, d), ln_b.reshape(1, d), router_w)


def _post_moe_body(s0_ref, s1_ref, x_ref, gate_ref, g_ref, b_ref, *rest, alpha, tm, tok0, has_next):
    if has_next:
        shift_ref, scale_ref, slots_hbm, xo_ref, hn_ref, buf, sem = rest
    else:
        slots_hbm, xo_ref, buf, sem = rest
    i = pl.program_id(0)

    def copies(t, slot, r):
        tok = tok0 + t * tm + r
        return (pltpu.make_async_copy(slots_hbm.at[pl.ds(s0_ref[tok], 1)], buf.at[slot, 0, pl.ds(r, 1)], sem.at[slot]),
                pltpu.make_async_copy(slots_hbm.at[pl.ds(s1_ref[tok], 1)], buf.at[slot, 1, pl.ds(r, 1)], sem.at[slot]))

    def issue_tile(t, slot):
        def issue(r, c):
            a, b = copies(t, slot, r)
            a.start()
            b.start()
            return c

        lax.fori_loop(0, tm, issue, 0)

    @pl.when(i == 0)
    def _():
        issue_tile(0, 0)

    @pl.when(i + 1 < pl.num_programs(0))
    def _():
        issue_tile(i + 1, (i + 1) % 2)

    slot = i % 2

    def wait(r, c):
        a, b = copies(i, slot, r)
        a.wait()
        b.wait()
        return c

    lax.fori_loop(0, tm, wait, 0)
    f = buf[slot, 0] + buf[slot, 1]
    xn = _layer_norm(alpha * x_ref[...] + gate_ref[...] * f, g_ref[...], b_ref[...])
    xo_ref[...] = xn
    if has_next:
        hn_ref[...] = (xn * (1.0 + scale_ref[...]) + shift_ref[...]).astype(hn_ref.dtype)


def post_moe(x, slots, s0, s1, tok0, mods, mods_next, rows_per_batch, ln_g, ln_b, alpha, tm=128):
    m, d = x.shape
    tm = min(tm, m)
    has_next = mods_next is not None
    row = pl.BlockSpec((tm, d), lambda i, *_: (i, 0))
    vec = pl.BlockSpec((1, d), lambda i, *_: (0, 0))
    in_specs = [row] + _mod_specs(mods, (5,), tm, rows_per_batch, d) + [vec, vec]
    args = [x, mods, ln_g.reshape(1, d), ln_b.reshape(1, d)]
    out_shape = [jax.ShapeDtypeStruct((m, d), F32)]
    out_specs = [row]
    if has_next:
        in_specs += _mod_specs(mods_next, (0, 1), tm, rows_per_batch, d)
        args += [mods_next, mods_next]
        out_shape.append(jax.ShapeDtypeStruct((m, d), BF16))
        out_specs.append(row)
    in_specs.append(pl.BlockSpec(memory_space=pl.ANY))
    args.append(slots)
    out = pl.pallas_call(
        functools.partial(_post_moe_body, alpha=alpha, tm=tm, tok0=tok0, has_next=has_next),
        out_shape=tuple(out_shape),
        grid_spec=pltpu.PrefetchScalarGridSpec(
            num_scalar_prefetch=2, grid=(m // tm,), in_specs=in_specs, out_specs=tuple(out_specs),
            scratch_shapes=[pltpu.VMEM((2, 2, tm, d), F32), pltpu.SemaphoreType.DMA((2,))]),
        compiler_params=_params(("arbitrary",)),
        name="post_moe",
    )(s0, s1, *args)
    return out if has_next else (out[0], None)


def _route_body(lg_ref, bias_ref, ids_ref, w_ref, *, per_group):
    s = jax.nn.sigmoid(lg_ref[...])
    bz = s + bias_ref[...]
    n_exp = s.shape[0]
    rows_b = [bz[e:e + 1, :] for e in range(n_exp)]
    rows_s = [s[e:e + 1, :] for e in range(n_exp)]
    best = None
    for g in range(n_exp // per_group):
        members = range(g * per_group, (g + 1) * per_group)
        rank = {}
        for e in members:
            r = jnp.zeros_like(rows_b[e], dtype=jnp.int32)
            for o in members:
                if o == e:
                    continue
                ahead = (rows_b[o] > rows_b[e]) | ((rows_b[o] == rows_b[e]) & (o < e))
                r = r + ahead.astype(jnp.int32)
            rank[e] = r
        zero = jnp.zeros_like(rows_b[0])
        score = sum(jnp.where(rank[e] < 2, rows_b[e], zero) for e in members)
        e0 = sum(jnp.where(rank[e] == 0, e, 0) for e in members)
        e1 = sum(jnp.where(rank[e] == 1, e, 0) for e in members)
        s0 = sum(jnp.where(rank[e] == 0, rows_s[e], zero) for e in members)
        s1 = sum(jnp.where(rank[e] == 1, rows_s[e], zero) for e in members)
        if best is None:
            best = (score, e0, e1, s0, s1)
        else:
            take = score > best[0]
            best = tuple(jnp.where(take, new, old) for new, old in zip((score, e0, e1, s0, s1), best))
    _, e0, e1, s0, s1 = best
    tot = s0 + s1
    ids_ref[0:1, :] = e0
    ids_ref[1:2, :] = e1
    w_ref[0:1, :] = s0 / tot
    w_ref[1:2, :] = s1 / tot


def route(logits_t, router_bias):
    n_exp, n = logits_t.shape
    return pl.pallas_call(
        functools.partial(_route_body, per_group=n_exp // N_EXPERT_GROUPS),
        out_shape=(jax.ShapeDtypeStruct((2, n), jnp.int32), jax.ShapeDtypeStruct((2, n), F32)),
        name="route",
    )(logits_t, router_bias.reshape(n_exp, 1).astype(F32))


def _dispatch_body(tok_ref, nused_ref, src_a, src_b, o_ref, buf, sem, *, tm, n_a):
    i = pl.program_id(0)
    n_used = nused_ref[0]

    def row_copy(src, row, slot, r):
        return pltpu.make_async_copy(src.at[pl.ds(row, 1)], buf.at[slot, pl.ds(r, 1)], sem.at[slot])

    def issue_tile(t, slot):
        def issue(r, c):
            tok = tok_ref[t * tm + r]

            @pl.when(tok < n_a)
            def _():
                row_copy(src_a, tok, slot, r).start()

            @pl.when(tok >= n_a)
            def _():
                row_copy(src_b, tok - n_a, slot, r).start()

            return c

        lax.fori_loop(0, tm, issue, 0)

    @pl.when(i == 0)
    def _():
        issue_tile(0, 0)

    @pl.when(i + 1 < n_used)
    def _():
        issue_tile(i + 1, (i + 1) % 2)

    @pl.when(i < n_used)
    def _():
        slot = i % 2

        def wait(r, c):
            row_copy(src_a, 0, slot, r).wait()
            return c

        lax.fori_loop(0, tm, wait, 0)
        o_ref[...] = buf[slot].astype(o_ref.dtype)

    @pl.when(i >= n_used)
    def _():
        o_ref[...] = jnp.zeros_like(o_ref)


def dispatch(h_a, h_b, slot_tok, n_used, n_slots, tm):
    n_a, d = h_a.shape
    return pl.pallas_call(
        functools.partial(_dispatch_body, tm=tm, n_a=n_a),
        out_shape=jax.ShapeDtypeStruct((n_slots, d), BF16),
        grid_spec=pltpu.PrefetchScalarGridSpec(
            num_scalar_prefetch=2, grid=(n_slots // tm,),
            in_specs=[pl.BlockSpec(memory_space=pl.ANY), pl.BlockSpec(memory_space=pl.ANY)],
            out_specs=pl.BlockSpec((tm, d), lambda i, tok, nu: (i, 0)),
            scratch_shapes=[pltpu.VMEM((2, tm, d), F32), pltpu.SemaphoreType.DMA((2,))]),
        compiler_params=_params(("arbitrary",)),
        name="dispatch",
    )(slot_tok, n_used, h_a, h_b)


def _gate_up_body(te_ref, first_ref, nused_ref, x_ref, wg_ref, wu_ref, sw_ref, o_ref, wg_bf, wu_bf):
    t = pl.program_id(1)

    @pl.when(first_ref[t] == 1)
    def _():
        wg_bf[...] = wg_ref[...].astype(BF16)
        wu_bf[...] = wu_ref[...].astype(BF16)

    @pl.when(t < nused_ref[0])
    def _():
        x = x_ref[...]
        hg = jnp.dot(x, wg_bf[...], preferred_element_type=F32)
        hu = jnp.dot(x, wu_bf[...], preferred_element_type=F32)
        act = hg * jax.nn.sigmoid(hg) * hu * sw_ref[...]
        o_ref[...] = act.astype(o_ref.dtype)

    @pl.when(t >= nused_ref[0])
    def _():
        o_ref[...] = jnp.zeros_like(o_ref)


def expert_gate_up(xs, w_gate, w_up, layer, slot_w, tile_e, tile_first, n_used, tm, tf=512):
    n_slots, d = xs.shape
    f = w_gate.shape[3]
    tf = min(tf, f)
    n_tiles = n_slots // tm

    def tile(t, nu):
        return jnp.minimum(t, nu[0] - 1)

    w_spec = pl.BlockSpec((None, None, d, tf), lambda j, t, te, fi, nu: (layer, te[t], 0, j))
    return pl.pallas_call(
        _gate_up_body,
        out_shape=jax.ShapeDtypeStruct((n_slots, f), BF16),
        grid_spec=pltpu.PrefetchScalarGridSpec(
            num_scalar_prefetch=3, grid=(f // tf, n_tiles),
            in_specs=[pl.BlockSpec((tm, d), lambda j, t, te, fi, nu: (tile(t, nu), 0)), w_spec, w_spec,
                      pl.BlockSpec((tm, 1), lambda j, t, te, fi, nu: (tile(t, nu), 0))],
            out_specs=pl.BlockSpec((tm, tf), lambda j, t, te, fi, nu: (t, j)),
            scratch_shapes=[pltpu.VMEM((d, tf), BF16), pltpu.VMEM((d, tf), BF16)]),
        compiler_params=_params(("arbitrary", "arbitrary")),
        name="expert_gate_up",
    )(tile_e, tile_first, n_used, xs, w_gate, w_up, slot_w)


def _down_body(te_ref, first_ref, nused_ref, a_ref, w_ref, o_ref, w_bf):
    t = pl.program_id(1)

    @pl.when(first_ref[t] == 1)
    def _():
        w_bf[...] = w_ref[...].astype(BF16)

    @pl.when(t < nused_ref[0])
    def _():
        o_ref[...] = jnp.dot(a_ref[...], w_bf[...], preferred_element_type=F32)

    @pl.when(t >= nused_ref[0])
    def _():
        o_ref[...] = jnp.zeros_like(o_ref)


def expert_down(act, w_down, layer, tile_e, tile_first, n_used, tm, tn=2048):
    n_slots, f = act.shape
    d = w_down.shape[3]
    tn = min(tn, d)
    n_tiles = n_slots // tm

    def tile(t, nu):
        return jnp.minimum(t, nu[0] - 1)

    return pl.pallas_call(
        _down_body,
        out_shape=jax.ShapeDtypeStruct((n_slots, d), F32),
        grid_spec=pltpu.PrefetchScalarGridSpec(
            num_scalar_prefetch=3, grid=(d // tn, n_tiles),
            in_specs=[pl.BlockSpec((tm, f), lambda j, t, te, fi, nu: (tile(t, nu), 0)),
                      pl.BlockSpec((None, None, f, tn), lambda j, t, te, fi, nu: (layer, te[t], 0, j))],
            out_specs=pl.BlockSpec((tm, tn), lambda j, t, te, fi, nu: (t, j)),
            scratch_shapes=[pltpu.VMEM((f, tn), BF16)]),
        compiler_params=_params(("arbitrary", "arbitrary")),
        name="expert_down",
    )(tile_e, tile_first, n_used, act, w_down)


def moe_plan(ids, wts, n_exp, tm):
    n = ids.shape[1]
    n_pairs = 2 * n
    n_tiles = -(-(n_pairs + n_exp * (tm - 1)) // tm)
    n_slots = n_tiles * tm
    e_flat = ids.reshape(-1)
    onehot = (e_flat[:, None] == jnp.arange(n_exp, dtype=jnp.int32)[None, :]).astype(jnp.int32)
    csum = jnp.cumsum(onehot, axis=0)
    rank = jnp.sum(onehot * csum, axis=1) - 1
    counts = csum[-1]
    padded = ((counts + tm - 1) // tm) * tm
    ends = jnp.cumsum(padded)
    starts = ends - padded
    slot = (jnp.sum(onehot * starts[None, :], axis=1) + rank).astype(jnp.int32)
    tok = jnp.tile(jnp.arange(n, dtype=jnp.int32), 2)
    slot_tok = jnp.zeros((n_slots,), jnp.int32).at[slot].set(tok)
    slot_w = jnp.zeros((n_slots,), F32).at[slot].set(wts.reshape(-1))
    n_used = (ends[-1] // tm).astype(jnp.int32)
    tile_start = jnp.arange(n_tiles, dtype=jnp.int32) * tm
    tile_start = jnp.minimum(tile_start, ends[-1] - 1)
    tile_e = jnp.minimum(jnp.sum((tile_start[:, None] >= ends[None, :]).astype(jnp.int32), axis=1), n_exp - 1)
    tile_first = jnp.concatenate([jnp.ones((1,), jnp.int32), (tile_e[1:] != tile_e[:-1]).astype(jnp.int32)])
    return dict(n_slots=n_slots, slot_tok=slot_tok, slot_w=slot_w.reshape(n_slots, 1), n_used=n_used.reshape(1),
                tile_e=tile_e.astype(jnp.int32), tile_first=tile_first, s0=slot[:n], s1=slot[n:])


def moe_experts(h_a, h_b, logits, router_bias, w_gate, w_up, w_down, layer, tm=256):
    n_exp = w_gate.shape[1]
    ids, wts = route(logits.T, router_bias)
    plan = moe_plan(ids, wts, n_exp, tm)
    xs = dispatch(h_a, h_b, plan["slot_tok"], plan["n_used"], plan["n_slots"], tm)
    act = expert_gate_up(xs, w_gate, w_up, layer, plan["slot_w"], plan["tile_e"], plan["tile_first"],
                         plan["n_used"], tm)
    out = expert_down(act, w_down, layer, plan["tile_e"], plan["tile_first"], plan["n_used"], tm)
    return out, plan["s0"], plan["s1"]


NEG_INF = float("-inf")
_NT = (((1,), (1,)), ((), ()))


def _heads_first(vals, n_kv):
    lead = vals.ndim - 1
    v = jnp.transpose(vals, (lead,) + tuple(range(lead)))
    return v.reshape((n_kv, v.shape[0] // n_kv) + v.shape[1:])


def _toeplitz(vecs, n):
    lead = vecs.shape[:-1]
    u = jnp.pad(jnp.flip(vecs, -1), [(0, 0)] * len(lead) + [(0, 1)])
    rows = jnp.broadcast_to(u[..., None, :], lead + (n, 2 * n)).reshape(lead + (2 * n * n,))
    skew = rows[..., :n * (2 * n - 1)].reshape(lead + (n, 2 * n - 1))
    return skew[..., n - 1:]


def band_bias(rel_table, n_kv, max_step, dil):
    w = DIL_BAND
    steps = jnp.arange(-(2 * w - 1), 2 * w)
    vals = rel_table.astype(F32)[rel_bucket(steps * dil)]
    vals = jnp.where(((steps >= 0) & (steps <= max_step))[:, None], vals, NEG_INF)
    v = _toeplitz(_heads_first(vals, n_kv), 2 * w)[:, :, w:, :]
    return v.reshape(n_kv, v.shape[1] * w, 2 * w)


def moba_bias(rel_table, n_kv, n_blocks):
    blk = MOBA_BLOCK
    dist = jnp.arange(n_blocks)[:, None] * blk + jnp.arange(-(blk - 1), blk)[None, :]
    vals = rel_table.astype(F32)[rel_bucket(dist)]
    vals = jnp.where((dist >= 0)[..., None], vals, NEG_INF)
    v = jnp.transpose(_heads_first(vals, n_kv), (0, 2, 1, 3))
    v = _toeplitz(v, blk)
    return v.reshape(n_kv, n_blocks, v.shape[2] * blk, blk)


def _stack_heads(q_ref, r0, rows, g_heads):
    return jnp.concatenate([q_ref[pl.ds(r0, rows), g * HEAD_DIM:(g + 1) * HEAD_DIM] for g in range(g_heads)], axis=0)


def _band_body(*refs, n_blocks, g_heads, with_sink):
    if with_sink:
        q_ref, k_ref, v_ref, bias_ref, sink_ref, o_ref, kb, vb = refs
        lse_ref = None
    else:
        q_ref, k_ref, v_ref, bias_ref, o_ref, lse_ref, kb, vb = refs
    w = DIL_BAND
    kb[...] = k_ref[...].astype(BF16)
    vb[...] = v_ref[...].astype(BF16)

    def block(n, first):
        r0 = 0 if first else pl.multiple_of(n * w, w)
        qs = _stack_heads(q_ref, r0, w, g_heads)
        if first:
            keys, vals, bias = kb[0:w, :], vb[0:w, :], bias_ref[:, w:]
        else:
            keys, vals, bias = kb[pl.ds(r0 - w, 2 * w), :], vb[pl.ds(r0 - w, 2 * w), :], bias_ref[...]
        s = lax.dot_general(qs, keys, _NT, preferred_element_type=F32) + bias
        m = jnp.max(s, axis=1, keepdims=True)
        if with_sink:
            sink = sink_ref[...]
            m = jnp.maximum(m, sink)
            p = jnp.exp(s - m)
            den = jnp.sum(p, axis=1, keepdims=True) + jnp.exp(sink - m)
            p = p * (1.0 / den)
        else:
            lse = jnp.log(jnp.sum(jnp.exp(s - m), axis=1, keepdims=True)) + m
            p = jnp.exp(s - lse)
        out = jnp.dot(p.astype(BF16), vals, preferred_element_type=F32)
        for g in range(g_heads):
            cols = slice(g * HEAD_DIM, (g + 1) * HEAD_DIM)
            o_ref[pl.ds(r0, w), cols] = out[g * w:(g + 1) * w].astype(o_ref.dtype)
            if lse_ref is not None:
                lse_ref[pl.ds(r0, w), cols] = jnp.broadcast_to(lse[g * w:(g + 1) * w], (w, HEAD_DIM))

    block(0, True)
    if n_blocks > 1:
        def body(n, c):
            block(n, False)
            return c
        lax.fori_loop(1, n_blocks, body, 0)


def band_attention(q2, kv2, bias, sinks, *, batch, dil, n_kv, k_col0, v_col0, kv_cols, out_dtype):
    n_tok, qd = q2.shape
    seq = n_tok // batch
    m = seq // dil
    g_heads = qd // HEAD_DIM // n_kv
    gw = g_heads * HEAD_DIM
    assert m % DIL_BAND == 0
    qv = q2.reshape(batch * m, dil * qd)
    kvv = kv2.reshape(batch * m, dil * kv_cols)
    q_per = qd // gw
    kv_per = kv_cols // HEAD_DIM
    kc, vc = k_col0 // HEAD_DIM, v_col0 // HEAD_DIM
    with_sink = sinks is not None
    q_spec = pl.BlockSpec((m, gw), lambda k, r, b: (b, r * q_per + k))
    in_specs = [q_spec,
                pl.BlockSpec((m, HEAD_DIM), lambda k, r, b: (b, r * kv_per + kc + k)),
                pl.BlockSpec((m, HEAD_DIM), lambda k, r, b: (b, r * kv_per + vc + k)),
                pl.BlockSpec((None, gw, 2 * DIL_BAND), lambda k, r, b: (k, 0, 0))]
    args = [qv, kvv, kvv, bias]
    out_shape = [jax.ShapeDtypeStruct(qv.shape, out_dtype)]
    out_specs = [q_spec]
    if with_sink:
        in_specs.append(pl.BlockSpec((None, gw, 1), lambda k, r, b: (k, 0, 0)))
        args.append(jnp.repeat(sinks.astype(F32), DIL_BAND).reshape(n_kv, gw, 1))
    else:
        out_shape.append(jax.ShapeDtypeStruct(qv.shape, F32))
        out_specs.append(q_spec)
    out = pl.pallas_call(
        functools.partial(_band_body, n_blocks=m // DIL_BAND, g_heads=g_heads, with_sink=with_sink),
        out_shape=tuple(out_shape),
        grid=(n_kv, dil, batch),
        in_specs=in_specs,
        out_specs=tuple(out_specs),
        scratch_shapes=[pltpu.VMEM((m, HEAD_DIM), BF16), pltpu.VMEM((m, HEAD_DIM), BF16)],
        compiler_params=_params(("arbitrary", "arbitrary", "arbitrary")),
        name="band_attention",
    )(*args)
    return tuple(o.reshape(n_tok, qd) for o in out)


def _merge_body(o0, o1, o2, l0, l1, l2, out_ref):
    a, b, c = l0[...], l1[...], l2[...]
    mx = jnp.maximum(jnp.maximum(a, b), c)
    ea, eb, ec = jnp.exp(a - mx), jnp.exp(b - mx), jnp.exp(c - mx)
    inv = 1.0 / (ea + eb + ec)
    out_ref[...] = ((ea * inv) * o0[...] + (eb * inv) * o1[...] + (ec * inv) * o2[...]).astype(out_ref.dtype)


def merge_groups(outs, lses, tm=128):
    m, d = outs[0].shape
    tm = min(tm, m)
    row = pl.BlockSpec((tm, d), lambda i: (i, 0))
    return pl.pallas_call(
        _merge_body,
        out_shape=jax.ShapeDtypeStruct((m, d), BF16),
        grid=(m // tm,),
        in_specs=[row] * 6,
        out_specs=row,
        compiler_params=_params(("arbitrary",)),
        name="merge_groups",
    )(*outs, *lses)


def _moba_body(q_ref, k_ref, v_ref, bias_ref, o_ref, kb, vb, ls, *, n_blocks, g_heads):
    blk = MOBA_BLOCK
    rows = g_heads * blk
    kf = k_ref[...]
    kb[...] = kf.astype(BF16)
    vb[...] = v_ref[...].astype(BF16)
    kmean = jnp.concatenate(
        [jnp.sum(kf[n * blk:(n + 1) * blk], axis=0, keepdims=True) * (1.0 / blk) for n in range(n_blocks)],
        axis=0).astype(BF16)
    lane = lax.broadcasted_iota(jnp.int32, (blk, n_blocks), 1)
    lane_s = lax.broadcasted_iota(jnp.int32, (rows, n_blocks), 1)

    def own_block(ob, carry):
        r0 = pl.multiple_of(ob * blk, blk)
        qs = _stack_heads(q_ref, r0, blk, g_heads)
        gate_h = lax.dot_general(qs, kmean, _NT, preferred_element_type=F32)
        gate = gate_h[0:blk]
        for g in range(1, g_heads):
            gate = gate + gate_h[g * blk:(g + 1) * blk]
        gate = jnp.where(lane < ob, gate, NEG_INF)
        rank = jnp.zeros((blk, n_blocks), jnp.int32)
        for m_ in range(n_blocks):
            col = gate[:, m_:m_ + 1]
            rank = rank + ((col > gate) | ((col == gate) & (lane > m_))).astype(jnp.int32)
        sel = ((lane < ob) & (rank < MOBA_TOPK)).astype(F32)
        sel = jnp.concatenate([sel] * g_heads, axis=0)

        def logits(n, mrow):
            keys = kb[pl.ds(pl.multiple_of(n * blk, blk), blk), :]
            s = lax.dot_general(qs, keys, _NT, preferred_element_type=F32) + bias_ref[ob - n]
            chosen = jnp.sum(jnp.where(lane_s == n, sel, 0.0), axis=1, keepdims=True)
            s = jnp.where((chosen > 0.0) | (n == ob), s, NEG_INF)
            ls[n] = s
            return jnp.maximum(mrow, jnp.maximum(s[:, :HEAD_DIM], s[:, HEAD_DIM:]))

        mrow = lax.fori_loop(0, ob + 1, logits, jnp.full((rows, HEAD_DIM), NEG_INF, F32))
        m = jnp.max(mrow, axis=1, keepdims=True)

        def expo(n, lrow):
            e = jnp.exp(ls[n] - m)
            ls[n] = e
            return lrow + (e[:, :HEAD_DIM] + e[:, HEAD_DIM:])

        lrow = lax.fori_loop(0, ob + 1, expo, jnp.zeros((rows, HEAD_DIM), F32))
        inv = 1.0 / jnp.sum(lrow, axis=1, keepdims=True)

        def weigh(n, acc):
            p = (ls[n] * inv).astype(BF16)
            vals = vb[pl.ds(pl.multiple_of(n * blk, blk), blk), :]
            return acc + jnp.dot(p, vals, preferred_element_type=F32)

        acc = lax.fori_loop(0, ob + 1, weigh, jnp.zeros((rows, HEAD_DIM), F32))
        for g in range(g_heads):
            o_ref[pl.ds(r0, blk), g * HEAD_DIM:(g + 1) * HEAD_DIM] = acc[g * blk:(g + 1) * blk].astype(o_ref.dtype)
        return carry

    lax.fori_loop(0, n_blocks, own_block, 0)


def moba_prompt(q2, kv2, bias, *, batch, n_kv):
    n_tok, qd = q2.shape
    seq = n_tok // batch
    n_blocks = seq // MOBA_BLOCK
    g_heads = qd // HEAD_DIM // n_kv
    gw = g_heads * HEAD_DIM
    q_spec = pl.BlockSpec((seq, gw), lambda k, b: (b, k))
    return pl.pallas_call(
        functools.partial(_moba_body, n_blocks=n_blocks, g_heads=g_heads),
        out_shape=jax.ShapeDtypeStruct((n_tok, qd), BF16),
        grid=(n_kv, batch),
        in_specs=[q_spec,
                  pl.BlockSpec((seq, HEAD_DIM), lambda k, b: (b, k)),
                  pl.BlockSpec((seq, HEAD_DIM), lambda k, b: (b, n_kv + k)),
                  pl.BlockSpec((None, n_blocks, g_heads * MOBA_BLOCK, MOBA_BLOCK), lambda k, b: (k, 0, 0, 0))],
        out_specs=q_spec,
        scratch_shapes=[pltpu.VMEM((seq, HEAD_DIM), BF16), pltpu.VMEM((seq, HEAD_DIM), BF16),
                        pltpu.VMEM((n_blocks, g_heads * MOBA_BLOCK, MOBA_BLOCK), F32)],
        compiler_params=_params(("arbitrary", "arbitrary")),
        name="moba_prompt",
    )(q2, kv2, kv2, bias)


def _rows_of_kv_head(n_heads, n_kv, k):
    g = n_heads // n_kv
    row = lax.broadcasted_iota(jnp.int32, (n_heads, 1), 0)
    return (row >= k * g) & (row < (k + 1) * g)


def _per_kv_head_dot(lhs, rhs_of, n_kv, nt):
    n_heads = lhs.shape[0]
    out = None
    for k in range(n_kv):
        rhs = rhs_of(k)
        if nt:
            r = lax.dot_general(lhs, rhs, _NT, preferred_element_type=F32)
        else:
            r = jnp.dot(lhs, rhs, preferred_element_type=F32)
        out = r if out is None else jnp.where(_rows_of_kv_head(n_heads, n_kv, k), r, out)
    return out


def _swa_sample_body(q_ref, kv_ref, bias_ref, sink_ref, o_ref, *, n_kv):
    q = q_ref[...]
    s = _per_kv_head_dot(q, lambda k: kv_ref[:, k * HEAD_DIM:(k + 1) * HEAD_DIM].astype(BF16), n_kv, True)
    s = s + bias_ref[...]
    sink = sink_ref[...]
    m = jnp.maximum(jnp.max(s, axis=1, keepdims=True), sink)
    p = jnp.exp(s - m)
    den = jnp.sum(p, axis=1, keepdims=True) + jnp.exp(sink - m)
    p = (p * (1.0 / den)).astype(BF16)
    o = _per_kv_head_dot(p, lambda k: kv_ref[:, (n_kv + k) * HEAD_DIM:(n_kv + k + 1) * HEAD_DIM].astype(BF16),
                         n_kv, False)
    o_ref[...] = o.astype(o_ref.dtype)


def swa_sample_attention(q3, window_kv, rel_table, sinks, n_kv):
    bsz, n_heads, _ = q3.shape
    w = window_kv.shape[1]
    bias = rel_table.astype(F32)[rel_bucket(w - 1 - jnp.arange(w))].T
    return pl.pallas_call(
        functools.partial(_swa_sample_body, n_kv=n_kv),
        out_shape=jax.ShapeDtypeStruct(q3.shape, BF16),
        grid=(bsz,),
        in_specs=[pl.BlockSpec((None, n_heads, HEAD_DIM), lambda b: (b, 0, 0)),
                  pl.BlockSpec((None, w, window_kv.shape[2]), lambda b: (b, 0, 0)),
                  pl.BlockSpec((n_heads, w), lambda b: (0, 0)),
                  pl.BlockSpec((n_heads, 1), lambda b: (0, 0))],
        out_specs=pl.BlockSpec((None, n_heads, HEAD_DIM), lambda b: (b, 0, 0)),
        compiler_params=_params(("arbitrary",)),
        name="swa_sample",
    )(q3, window_kv, bias, sinks.astype(F32).reshape(n_heads, 1))


def _with_new_row(old, new_row):
    pad = jnp.where(lax.broadcasted_iota(jnp.int32, (8, HEAD_DIM), 0) == 0,
                    jnp.broadcast_to(new_row, (8, HEAD_DIM)), 0.0)
    return jnp.concatenate([old, pad], axis=0).astype(BF16)


def _dil_sample_body(q_ref, b0_ref, b1_ref, b2_ref, new_ref, bias_ref, o_ref, *, n_kv):
    q = q_ref[...]
    gcols = 2 * n_kv * HEAD_DIM
    outs, lses = [], []
    for gi, buf in enumerate((b0_ref, b1_ref, b2_ref)):
        def keys(k, buf=buf, gi=gi):
            c = gi * gcols + k * HEAD_DIM
            return _with_new_row(buf[:, 0, k, :], new_ref[:, c:c + HEAD_DIM])

        def vals(k, buf=buf, gi=gi):
            c = gi * gcols + (n_kv + k) * HEAD_DIM
            return _with_new_row(buf[:, 1, k, :], new_ref[:, c:c + HEAD_DIM])

        s = _per_kv_head_dot(q, keys, n_kv, True) + bias_ref[gi]
        m = jnp.max(s, axis=1, keepdims=True)
        lse = jnp.log(jnp.sum(jnp.exp(s - m), axis=1, keepdims=True)) + m
        p = jnp.exp(s - lse).astype(BF16)
        outs.append(_per_kv_head_dot(p, vals, n_kv, False))
        lses.append(lse)
    mx = jnp.maximum(jnp.maximum(lses[0], lses[1]), lses[2])
    e = [jnp.exp(l - mx) for l in lses]
    inv = 1.0 / (e[0] + e[1] + e[2])
    o_ref[...] = ((e[0] * inv) * outs[0] + (e[1] * inv) * outs[1] + (e[2] * inv) * outs[2]).astype(o_ref.dtype)


def dilated_sample_attention(q3, bufs, new_kv, rel_table, n_kv):
    bsz, n_heads, _ = q3.shape
    gcols = 2 * n_kv * HEAD_DIM
    steps = jnp.arange(DIL_BAND, 0, -1)
    bias = []
    for _, dil in DIL_PAIRS:
        old = rel_table.astype(F32)[rel_bucket(dil * steps)].T
        new = rel_table.astype(F32)[rel_bucket(jnp.zeros((1,), jnp.int32))].T
        bias.append(jnp.concatenate([old, new, jnp.full((n_heads, 7), NEG_INF, F32)], axis=1))
    bias = jnp.stack(bias)
    views = []
    for (window, dil), buf in zip(DIL_PAIRS, bufs):
        assert buf.shape[1] == window == DIL_BAND * dil
        views.append(buf.reshape((bsz, DIL_BAND, dil) + buf.shape[2:]))
    buf_spec = pl.BlockSpec((None, DIL_BAND, None, 2, n_kv, HEAD_DIM), lambda b: (b, 0, 0, 0, 0, 0))
    return pl.pallas_call(
        functools.partial(_dil_sample_body, n_kv=n_kv),
        out_shape=jax.ShapeDtypeStruct(q3.shape, BF16),
        grid=(bsz,),
        in_specs=[pl.BlockSpec((None, n_heads, HEAD_DIM), lambda b: (b, 0, 0)), buf_spec, buf_spec, buf_spec,
                  pl.BlockSpec((None, 1, new_kv.shape[1]), lambda b: (b, 0, 0)),
                  pl.BlockSpec(bias.shape, lambda b: (0, 0, 0))],
        out_specs=pl.BlockSpec((None, n_heads, HEAD_DIM), lambda b: (b, 0, 0)),
        compiler_params=_params(("arbitrary",)),
        name="dilated_sample",
    )(q3, *views, new_kv.reshape(bsz, 1, -1), bias)


def _shift_append_body(buf_hbm, new_hbm, out_hbm, sem, *, slot):
    bsz, rows = out_hbm.shape[0], out_hbm.shape[1]

    def copies(b):
        return (pltpu.make_async_copy(buf_hbm.at[slot, b, pl.ds(1, rows - 1)], out_hbm.at[b, pl.ds(0, rows - 1)],
                                      sem.at[0]),
                pltpu.make_async_copy(new_hbm.at[b], out_hbm.at[b, pl.ds(rows - 1, 1)], sem.at[1]))

    for b in range(bsz):
        for c in copies(b):
            c.start()
    for b in range(bsz):
        for c in copies(b):
            c.wait()


def shift_append(bufs, slot, new_rows):
    assert new_rows.shape[1] == 1 and new_rows.shape[2:] == bufs.shape[3:]
    any_spec = pl.BlockSpec(memory_space=pl.ANY)
    return pl.pallas_call(
        functools.partial(_shift_append_body, slot=slot),
        out_shape=jax.ShapeDtypeStruct(bufs.shape[1:], bufs.dtype),
        in_specs=[any_spec, any_spec],
        out_specs=any_spec,
        scratch_shapes=[pltpu.SemaphoreType.DMA((2,))],
        name="shift_append",
    )(bufs, new_rows)


PAGES_PER_STEP = 4


def _kmean_body(pt_ref, *refs, pages_per_block):
    page_refs, o_ref = refs[:-1], refs[-1]
    j = pl.program_id(1)
    blocks_per_step = len(page_refs) // pages_per_block
    for i in range(blocks_per_step):
        tot = None
        for p in page_refs[i * pages_per_block:(i + 1) * pages_per_block]:
            part = jnp.sum(p[...], axis=0)
            tot = part if tot is None else tot + part
        o_ref[j * blocks_per_step + i] = tot * (1.0 / MOBA_BLOCK)


def paged_block_means(pages, slot, page_table):
    page, n_kv = pages.shape[2], pages.shape[4]
    bsz, n_pages = page_table.shape
    pages_per_block = MOBA_BLOCK // page
    assert n_pages % PAGES_PER_STEP == 0 and PAGES_PER_STEP % pages_per_block == 0
    n_blocks = n_pages // pages_per_block
    specs = [pl.BlockSpec((None, None, page, None, n_kv, HEAD_DIM), functools.partial(
        lambda b, j, pt, i: (slot, pt[b * n_pages + j * PAGES_PER_STEP + i], 0, 0, 0, 0), i=i))
        for i in range(PAGES_PER_STEP)]
    return pl.pallas_call(
        functools.partial(_kmean_body, pages_per_block=pages_per_block),
        out_shape=jax.ShapeDtypeStruct((bsz, n_blocks, n_kv, HEAD_DIM), F32),
        grid_spec=pltpu.PrefetchScalarGridSpec(
            num_scalar_prefetch=1, grid=(bsz, n_pages // PAGES_PER_STEP), in_specs=specs,
            out_specs=pl.BlockSpec((None, n_blocks, n_kv, HEAD_DIM), lambda b, j, pt: (b, 0, 0, 0))),
        compiler_params=_params(("arbitrary", "arbitrary")),
        name="paged_block_means",
    )(page_table.reshape(-1), *([pages] * PAGES_PER_STEP))


def _moba_select_body(q_ref, km_ref, sel_ref, *, n_kv):
    q = q_ref[...]
    n_heads = q.shape[0]
    n_blocks = km_ref.shape[1]
    rows = []
    for k in range(n_kv):
        km = km_ref[k].astype(BF16)
        g_all = lax.dot_general(q, km, _NT, preferred_element_type=F32)
        rows.append(jnp.sum(jnp.where(_rows_of_kv_head(n_heads, n_kv, k), g_all, 0.0), axis=0, keepdims=True))
    gate = jnp.concatenate(rows, axis=0)
    lane = lax.broadcasted_iota(jnp.int32, gate.shape, 1)
    rank = jnp.zeros(gate.shape, jnp.int32)
    for m_ in range(n_blocks):
        col = gate[:, m_:m_ + 1]
        rank = rank + ((col > gate) | ((col == gate) & (lane > m_))).astype(jnp.int32)
    out_lane = lax.broadcasted_iota(jnp.int32, sel_ref.shape, 1)
    out = jnp.zeros(sel_ref.shape, jnp.int32)
    for r in range(MOBA_TOPK):
        idx = jnp.sum(jnp.where(rank == r, lane, 0), axis=1, keepdims=True)
        out = jnp.where(out_lane == r, idx, out)
    sel_ref[...] = out


def moba_select(q3, kmean):
    bsz, n_heads, _ = q3.shape
    n_kv, n_blocks = kmean.shape[1], kmean.shape[2]
    assert n_blocks >= MOBA_TOPK
    sel = pl.pallas_call(
        functools.partial(_moba_select_body, n_kv=n_kv),
        out_shape=jax.ShapeDtypeStruct((bsz, n_kv, HEAD_DIM), jnp.int32),
        grid=(bsz,),
        in_specs=[pl.BlockSpec((None, n_heads, HEAD_DIM), lambda b: (b, 0, 0)),
                  pl.BlockSpec((None, n_kv, n_blocks, HEAD_DIM), lambda b: (b, 0, 0, 0))],
        out_specs=pl.BlockSpec((None, n_kv, HEAD_DIM), lambda b: (b, 0, 0)),
        compiler_params=_params(("arbitrary",)),
        name="moba_select",
    )(q3, kmean)
    return sel[:, :, :MOBA_TOPK]


def _moba_sample_body(pt_ref, sel_ref, q_ref, pb_ref, new_ref, b0_ref, pages_hbm, o_ref, kbuf, vbuf, sem,
                      *, slot, n_kv, n_pages, page):
    b = pl.program_id(0)

    def block_of(k, j):
        return sel_ref[(b * n_kv + k) * MOBA_TOPK + j]

    def copies(k, j):
        out = []
        for half in range(MOBA_BLOCK // page):
            pg = pt_ref[b * n_pages + (MOBA_BLOCK // page) * block_of(k, j) + half]
            rows = pl.ds(half * page, page)
            out.append(pltpu.make_async_copy(pages_hbm.at[slot, pg, pl.ds(0, page), 0, k, :],
                                             kbuf.at[k, j, rows, :], sem.at[0]))
            out.append(pltpu.make_async_copy(pages_hbm.at[slot, pg, pl.ds(0, page), 1, k, :],
                                             vbuf.at[k, j, rows, :], sem.at[1]))
        return out

    for k in range(n_kv):
        for j in range(MOBA_TOPK):
            for c in copies(k, j):
                c.start()
    for k in range(n_kv):
        for j in range(MOBA_TOPK):
            for c in copies(k, j):
                c.wait()

    for k in range(n_kv):
        q = q_ref[k]
        kn = new_ref[k].astype(BF16).astype(F32)
        vn = new_ref[n_kv + k].astype(BF16).astype(F32)
        s_new = jnp.sum(q.astype(F32) * kn, axis=1, keepdims=True) + b0_ref[k]
        ss = [lax.dot_general(q, kbuf[k, j].astype(BF16), _NT, preferred_element_type=F32) + pb_ref[k, block_of(k, j)]
              for j in range(MOBA_TOPK)]
        m = s_new
        for s in ss:
            m = jnp.maximum(m, jnp.max(s, axis=1, keepdims=True))
        e_new = jnp.exp(s_new - m)
        es = [jnp.exp(s - m) for s in ss]
        tot = e_new
        for e in es:
            tot = tot + jnp.sum(e, axis=1, keepdims=True)
        inv = 1.0 / tot
        own_out = (e_new * inv).astype(BF16).astype(F32) * vn
        sel_out = None
        for j in range(MOBA_TOPK):
            part = jnp.dot((es[j] * inv).astype(BF16), vbuf[k, j].astype(BF16), preferred_element_type=F32)
            sel_out = part if sel_out is None else sel_out + part
        o_ref[k] = (sel_out + own_out).astype(o_ref.dtype)


def moba_sample_attention(q3, pages, slot, page_table, sel, new_kv, rel_table):
    bsz, n_heads, _ = q3.shape
    page, n_kv = pages.shape[2], pages.shape[4]
    g = n_heads // n_kv
    n_pages = page_table.shape[1]
    assert MOBA_BLOCK % page == 0
    past_len = n_pages * page
    n_blocks = past_len // MOBA_BLOCK
    pos = jnp.arange(past_len, dtype=jnp.int32)
    pb = rel_table.astype(F32)[rel_bucket(past_len - pos)]
    pb = jnp.transpose(pb.reshape(n_blocks, MOBA_BLOCK, n_kv, g), (2, 0, 3, 1))
    b0 = rel_table.astype(F32)[rel_bucket(jnp.zeros((), jnp.int32))].reshape(n_kv, g, 1)
    head = pl.BlockSpec((None, n_kv, g, HEAD_DIM), lambda b, pt, sl: (b, 0, 0, 0))
    out = pl.pallas_call(
        functools.partial(_moba_sample_body, slot=slot, n_kv=n_kv, n_pages=n_pages, page=page),
        out_shape=jax.ShapeDtypeStruct((bsz, n_kv, g, HEAD_DIM), BF16),
        grid_spec=pltpu.PrefetchScalarGridSpec(
            num_scalar_prefetch=2, grid=(bsz,),
            in_specs=[head,
                      pl.BlockSpec(pb.shape, lambda b, pt, sl: (0, 0, 0, 0)),
                      pl.BlockSpec((None, 2 * n_kv, 1, HEAD_DIM), lambda b, pt, sl: (b, 0, 0, 0)),
                      pl.BlockSpec(b0.shape, lambda b, pt, sl: (0, 0, 0)),
                      pl.BlockSpec(memory_space=pl.ANY)],
            out_specs=head,
            scratch_shapes=[pltpu.VMEM((n_kv, MOBA_TOPK, MOBA_BLOCK, HEAD_DIM), F32),
                            pltpu.VMEM((n_kv, MOBA_TOPK, MOBA_BLOCK, HEAD_DIM), F32),
                            pltpu.SemaphoreType.DMA((2,))]),
        compiler_params=_params(("arbitrary",)),
        name="moba_sample",
    )(page_table.reshape(-1), sel.reshape(-1), q3.reshape(bsz, n_kv, g, HEAD_DIM), pb,
      new_kv.reshape(bsz, 2 * n_kv, 1, HEAD_DIM), b0, pages)
    return out.reshape(bsz, n_heads, HEAD_DIM)


def rel_bucket(dist):
    n = jnp.maximum(dist, 0)
    exact = REL_BUCKETS // 2
    scaled = (jnp.log(jnp.maximum(n, exact).astype(F32) / exact)
              / math.log(REL_MAX_DISTANCE / exact) * (REL_BUCKETS - exact))
    large = jnp.minimum(exact + scaled.astype(jnp.int32), REL_BUCKETS - 1)
    return jnp.where(n < exact, n, large)


def rel_bias(dist, rel_table):
    return rel_table.astype(F32)[rel_bucket(dist)]


def sink_softmax(logits, sink):
    m = jnp.maximum(jnp.max(logits, axis=-1, keepdims=True), sink)
    p = jnp.exp(logits - m)
    return p / (jnp.sum(p, axis=-1, keepdims=True) + jnp.exp(sink - m))


def swa_prompt(q, kv, sinks, rel_table):
    b, s = q.shape[:2]
    n_heads = q.shape[2]
    kvh = kv.shape[3]
    gqa = n_heads // kvh
    w = SWA_WINDOW
    nb = s // w
    qb = q.reshape(b, nb, w, kvh, gqa, HEAD_DIM)
    kvb = kv.reshape(b, nb, w, 2, kvh, HEAD_DIM)
    prev = jnp.pad(kvb, ((0, 0), (1, 0), (0, 0), (0, 0), (0, 0), (0, 0)))[:, :-1]
    band = jnp.concatenate([prev, kvb], axis=2)
    logits = jnp.einsum('bnqkgd,bnskd->bnkgqs', qb, band[:, :, :, 0]).astype(F32)
    dist = (jnp.arange(w)[:, None] + w) - jnp.arange(2 * w)[None, :]
    key_pos = jnp.arange(nb)[:, None, None] * w - w + jnp.arange(2 * w)[None, None, :]
    keep = (dist >= 0) & (dist < w) & (key_pos >= 0)
    bias = jnp.transpose(rel_bias(dist, rel_table), (2, 0, 1)).reshape(kvh, gqa, w, 2 * w)
    logits = jnp.where(keep[None, :, None, None], logits + bias, -jnp.inf)
    p = sink_softmax(logits, sinks.astype(F32).reshape(kvh, gqa, 1, 1))
    out = jnp.einsum('bnkgqs,bnskd->bnqkgd', p.astype(kv.dtype), band[:, :, :, 1])
    return out.reshape(b, s, n_heads, HEAD_DIM)


def swa_sample(q, kv_new, buf, sinks, rel_table):
    b, t = q.shape[:2]
    n_heads = q.shape[2]
    kvh = buf.shape[3]
    gqa = n_heads // kvh
    lb = buf.shape[1]
    kk = jnp.concatenate([buf, kv_new], axis=1)
    dist = (lb + jnp.arange(t))[:, None] - jnp.arange(lb + t)[None, :]
    keep = (dist >= 0) & (dist < SWA_WINDOW)
    qg = q.reshape(b, t, kvh, gqa, HEAD_DIM)
    logits = jnp.einsum('btkgd,bskd->bkgts', qg, kk[:, :, 0]).astype(F32)
    bias = jnp.transpose(rel_bias(dist, rel_table), (2, 0, 1)).reshape(kvh, gqa, t, lb + t)
    logits = jnp.where(keep, logits + bias, -jnp.inf)
    p = sink_softmax(logits, sinks.astype(F32).reshape(kvh, gqa, 1, 1))
    out = jnp.einsum('bkgts,bskd->btkgd', p.astype(kk.dtype), kk[:, :, 1]).reshape(b, t, n_heads, HEAD_DIM)
    new_len = min(SWA_WINDOW, lb + t)
    return out, kk[:, lb + t - new_len:]


def moba_attention(q, kv, q_pos, rel_table, q_chunk):
    b, sq = q.shape[:2]
    n_heads = q.shape[2]
    kvh = kv.shape[3]
    gqa = n_heads // kvh
    L = kv.shape[1]
    nb = -(-L // MOBA_BLOCK)
    kvb = jnp.pad(kv, ((0, 0), (0, nb * MOBA_BLOCK - L), (0, 0), (0, 0), (0, 0)))
    kvb = kvb.reshape(b, nb, MOBA_BLOCK, 2, kvh, HEAD_DIM)
    kmean = jnp.mean(kvb[:, :, :, 0], axis=2, dtype=F32)
    n_sel = min(MOBA_TOPK, nb)
    tab = rel_table.astype(F32).T.reshape(kvh, gqa, REL_BUCKETS)
    b_i = jnp.arange(b)[:, None, None, None]
    kv_i = jnp.arange(kvh)[None, None, :, None]
    t_i = jnp.arange(MOBA_BLOCK)

    def chunk(args):
        qc, pc = args
        qc = qc.reshape(b, q_chunk, kvh, gqa, HEAD_DIM)
        own = pc[0] // MOBA_BLOCK
        gate = jnp.einsum('bqkgd,bnkd->bqkn', qc.astype(F32), kmean)
        gate = jnp.where(jnp.arange(nb) < own, gate, -jnp.inf)
        _, blk = lax.top_k(gate, n_sel)
        sel = kvb[b_i, blk, :, :, kv_i]
        s_log = jnp.einsum('bqkgd,bqkntd->bqkgnt', qc, sel[..., 0, :]).astype(F32)
        s_dist = pc[None, :, None, None, None] - (blk[..., None] * MOBA_BLOCK + t_i)
        s_bias = tab[jnp.arange(kvh)[:, None, None, None], jnp.arange(gqa)[:, None, None],
                     rel_bucket(s_dist)[:, :, :, None]]
        s_keep = (blk < own)[:, :, :, None, :, None]
        s_log = jnp.where(s_keep, s_log + s_bias, -jnp.inf)
        own_kv = lax.dynamic_index_in_dim(kvb, own, axis=1, keepdims=False)
        o_log = jnp.einsum('bqkgd,btkd->bqkgt', qc, own_kv[:, :, 0]).astype(F32)
        o_dist = pc[:, None] - (own * MOBA_BLOCK + t_i)[None, :]
        o_bias = jnp.moveaxis(rel_bias(o_dist, rel_table), -1, 1).reshape(q_chunk, kvh, gqa, MOBA_BLOCK)
        o_log = jnp.where(o_dist[:, None, None, :] >= 0, o_log + o_bias, -jnp.inf)
        logits = jnp.concatenate([s_log.reshape(b, q_chunk, kvh, gqa, n_sel * MOBA_BLOCK), o_log], axis=-1)
        p = jax.nn.softmax(logits, axis=-1).astype(kv.dtype)
        p_sel = p[..., :n_sel * MOBA_BLOCK].reshape(b, q_chunk, kvh, gqa, n_sel, MOBA_BLOCK)
        out = (jnp.einsum('bqkgnt,bqkntd->bqkgd', p_sel, sel[..., 1, :])
               + jnp.einsum('bqkgt,btkd->bqkgd', p[..., n_sel * MOBA_BLOCK:], own_kv[:, :, 1]))
        return out.reshape(b, q_chunk, n_heads, HEAD_DIM)

    nc = sq // q_chunk
    qs = jnp.moveaxis(q.reshape(b, nc, q_chunk, n_heads, HEAD_DIM), 1, 0)
    ps = q_pos.reshape(nc, q_chunk)
    out = lax.map(chunk, (qs, ps))
    return jnp.moveaxis(out, 0, 1).reshape(b, sq, n_heads, HEAD_DIM)


def dilated_prompt(q, kv, dil, rel_table):
    b, s = q.shape[:2]
    n_heads = q.shape[2]
    kvh = kv.shape[3]
    gqa = n_heads // kvh
    m = s // dil
    mb = -(-m // DIL_BAND)
    mp = mb * DIL_BAND

    def to_sub(x):
        x = jnp.moveaxis(x.reshape((b, m, dil) + x.shape[2:]), 2, 1)
        x = jnp.pad(x, [(0, 0), (0, 0), (0, mp - m)] + [(0, 0)] * (x.ndim - 3))
        return x.reshape((b, dil, mb, DIL_BAND) + x.shape[3:])

    qs = to_sub(q.reshape(b, s, kvh, gqa, HEAD_DIM))
    kvs = to_sub(kv)
    prev = jnp.pad(kvs, ((0, 0), (0, 0), (1, 0), (0, 0), (0, 0), (0, 0), (0, 0)))[:, :, :-1]
    band = jnp.concatenate([prev, kvs], axis=3)
    logits = jnp.einsum('brnqkgd,brnskd->brnkgqs', qs, band[:, :, :, :, 0]).astype(F32)
    steps = (jnp.arange(DIL_BAND)[:, None] + DIL_BAND) - jnp.arange(2 * DIL_BAND)[None, :]
    sub_pos = jnp.arange(mb)[:, None, None] * DIL_BAND - DIL_BAND + jnp.arange(2 * DIL_BAND)[None, None, :]
    keep = (steps >= 0) & (steps <= DIL_BAND) & (sub_pos >= 0)
    bias = jnp.transpose(rel_bias(steps * dil, rel_table), (2, 0, 1)).reshape(kvh, gqa, DIL_BAND, 2 * DIL_BAND)
    logits = jnp.where(keep[None, None, :, None, None], logits + bias, -jnp.inf)
    lse = jax.nn.logsumexp(logits, axis=-1)
    p = jnp.exp(logits - lse[..., None]).astype(kv.dtype)
    out = jnp.einsum('brnkgqs,brnskd->brnqkgd', p, band[:, :, :, :, 1])
    out = jnp.moveaxis(out.reshape(b, dil, mp, n_heads, HEAD_DIM)[:, :, :m], 1, 2).reshape(b, s, n_heads, HEAD_DIM)
    lse = jnp.moveaxis(lse, -1, 3).reshape(b, dil, mp, n_heads)[:, :, :m]
    lse = jnp.moveaxis(lse, 1, 2).reshape(b, s, n_heads)
    return out, lse


def dilated_sample(q, kv_new, buf, dil, window, rel_table):
    b, t = q.shape[:2]
    n_heads = q.shape[2]
    kvh = buf.shape[3]
    gqa = n_heads // kvh
    lb = buf.shape[1]
    kk = jnp.concatenate([buf, kv_new], axis=1)
    k_steps = jnp.arange(DIL_BAND + 1)
    idx = lb + jnp.arange(t)[:, None] - dil * k_steps[None, :]
    keep = idx >= 0
    g = kk[:, jnp.maximum(idx, 0)]
    qg = q.reshape(b, t, kvh, gqa, HEAD_DIM)
    logits = jnp.einsum('btkgd,btjkd->btkgj', qg, g[:, :, :, 0]).astype(F32)
    bias = rel_bias(dil * k_steps, rel_table).T.reshape(kvh, gqa, DIL_BAND + 1)
    logits = jnp.where(keep[None, :, None, None, :], logits + bias, -jnp.inf)
    lse = jax.nn.logsumexp(logits, axis=-1)
    p = jnp.exp(logits - lse[..., None]).astype(kk.dtype)
    out = jnp.einsum('btkgj,btjkd->btkgd', p, g[:, :, :, 1]).reshape(b, t, n_heads, HEAD_DIM)
    new_len = min(window, lb + t)
    return out, lse.reshape(b, t, n_heads), kk[:, lb + t - new_len:]


def merge_by_denominator(outs, lses):
    wts = jax.nn.softmax(jnp.stack(lses), axis=0)
    return jnp.sum(wts[..., None] * jnp.stack(outs).astype(F32), axis=0)


def kernel(x_prompt, x_sample, cache_a_kv, cache_b_kv_pages, cache_c_kv_g0, cache_c_kv_g1, cache_c_kv_g2,
           page_table, c_prompt, c_sample, rel_table, ada_w, ada_b, ln_g, ln_b,
           a_w_qkv, a_w_o, a_sinks, b_w_qkv, b_w_o, c_w_qkv, c_w_o,
           router_w, router_bias, moe_w_gate, moe_w_up, moe_w_down):
    bp, seq, d = x_prompt.shape
    bs, dseq, _ = x_sample.shape
    depth = ada_w.shape[0]
    n_heads = d // HEAD_DIM
    qd = n_heads * HEAD_DIM
    kvh = n_heads // GQA
    dil_kvh = kvh // 2
    n_dil = len(DIL_PAIRS)
    alpha = (2 * depth) ** 0.25
    q_scale = HEAD_DIM ** -0.5
    mp, ms = bp * seq, bs * dseq
    past_len = page_table.shape[1] * cache_b_kv_pages.shape[2]

    n_c = bp + bs
    c_rows = -(-n_c // 8) * 8
    c_all = jnp.concatenate([c_prompt, c_sample, jnp.zeros((c_rows - n_c, d), F32)], axis=0)
    mods_all = adaln_all(c_all, ada_w, ada_b)

    def layer_mods(i):
        return mods_all[i, :bp].reshape(bp, 1, 6 * d), mods_all[i, bp:n_c]

    xp = x_prompt.reshape(mp, d)
    xs = x_sample.reshape(ms, d)
    mods_p, mods_s = layer_mods(0)
    hp = modulate(xp, mods_p, seq)
    hs = modulate(xs, mods_s, 1)

    bias_a = band_bias(rel_table, kvh, SWA_WINDOW - 1, 1)
    bias_b = moba_bias(rel_table, kvh, seq // MOBA_BLOCK)
    bias_c = [band_bias(rel_table, dil_kvh, DIL_BAND, dil) for _, dil in DIL_PAIRS]

    a_p, a_s, b_p, b_s = [], [], [], []
    c_p = [[] for _ in range(n_dil)]
    c_s = [[] for _ in range(n_dil)]
    slot_of = [0] * LAYER_KINDS
    for i in range(depth):
        kind = i % LAYER_KINDS
        slot = slot_of[kind]
        slot_of[kind] += 1
        w_qkv, w_o = ((a_w_qkv, a_w_o), (b_w_qkv, b_w_o), (c_w_qkv, c_w_o))[kind]
        kv_cols = w_qkv.shape[2] - qd
        qp = project(hp, w_qkv, slot, 0, qd, BF16, q_scale)
        kvp2 = project(hp, w_qkv, slot, qd, kv_cols, F32)
        qs = project(hs, w_qkv, slot, 0, qd, BF16, q_scale).reshape(bs, n_heads, HEAD_DIM)
        kvs2 = project(hs, w_qkv, slot, qd, kv_cols, F32)
        if kind == 0:
            (op,) = band_attention(qp, kvp2, bias_a, a_sinks[slot], batch=bp, dil=1, n_kv=kvh, k_col0=0,
                                   v_col0=kvh * HEAD_DIM, kv_cols=kv_cols, out_dtype=BF16)
            kvp = kvp2.reshape(bp, seq, 2, kvh, HEAD_DIM)
            kvs = kvs2.reshape(bs, dseq, 2, kvh, HEAD_DIM)
            assert cache_a_kv.shape[2] == SWA_WINDOW
            new_buf = shift_append(cache_a_kv, slot, kvs)
            os_ = swa_sample_attention(qs, new_buf.reshape(bs, SWA_WINDOW, kv_cols), rel_table, a_sinks[slot], kvh)
            a_p.append(kvp[:, seq - min(SWA_WINDOW, seq):])
            a_s.append(new_buf)
        elif kind == 1:
            op = moba_prompt(qp, kvp2, bias_b, batch=bp, n_kv=kvh)
            kmean = paged_block_means(cache_b_kv_pages, slot, page_table)
            sel = moba_select(qs, jnp.transpose(kmean, (0, 2, 1, 3)))
            os_ = moba_sample_attention(qs, cache_b_kv_pages, slot, page_table, sel, kvs2, rel_table)
            b_p.append(kvp2.reshape(bp, seq, 2, kvh, HEAD_DIM))
            b_s.append(kvs2.reshape(bs, dseq, 2, kvh, HEAD_DIM))
        else:
            kvp = kvp2.reshape(bp, seq, n_dil, 2, dil_kvh, HEAD_DIM)
            kvs = kvs2.reshape(bs, dseq, n_dil, 2, dil_kvh, HEAD_DIM)
            bufs = (cache_c_kv_g0[slot], cache_c_kv_g1[slot], cache_c_kv_g2[slot])
            op_l, lp_l = [], []
            group_cols = 2 * dil_kvh * HEAD_DIM
            for g, (window, dil) in enumerate(DIL_PAIRS):
                o, l = band_attention(qp, kvp2, bias_c[g], None, batch=bp, dil=dil, n_kv=dil_kvh,
                                      k_col0=g * group_cols, v_col0=g * group_cols + dil_kvh * HEAD_DIM,
                                      kv_cols=kv_cols, out_dtype=F32)
                op_l.append(o)
                lp_l.append(l)
                c_s[g].append(shift_append((cache_c_kv_g0, cache_c_kv_g1, cache_c_kv_g2)[g], slot, kvs[:, :, g]))
                c_p[g].append(kvp[:, seq - min(window, seq):, g])
            op = merge_groups(op_l, lp_l)
            os_ = dilated_sample_attention(qs, bufs, kvs2, rel_table, dil_kvh)
        yp = project(op, w_o, slot, 0, d, F32)
        ys = project(os_.reshape(ms, qd), w_o, slot, 0, d, F32)

        xp, h2p, lgp = post_attention(xp, yp, mods_p, seq, ln_g[i, 0], ln_b[i, 0], router_w, alpha)
        xs, h2s, lgs = post_attention(xs, ys, mods_s, 1, ln_g[i, 0], ln_b[i, 0], router_w, alpha)
        slots, s0, s1 = moe_experts(h2p, h2s, jnp.concatenate([lgp, lgs], axis=0),
                                    router_bias, moe_w_gate, moe_w_up, moe_w_down, i)
        if i + 1 < depth:
            nxt_p, nxt_s = layer_mods(i + 1)
        else:
            nxt_p = nxt_s = None
        xp, hp = post_moe(xp, slots, s0, s1, 0, mods_p, nxt_p, seq, ln_g[i, 1], ln_b[i, 1], alpha)
        xs, hs = post_moe(xs, slots, s0, s1, mp, mods_s, nxt_s, 1, ln_g[i, 1], ln_b[i, 1], alpha)
        mods_p, mods_s = nxt_p, nxt_s

    return (xp.reshape(bp, seq, d), xs.reshape(bs, dseq, d), jnp.stack(a_p), jnp.stack(a_s),
            jnp.stack(b_p), jnp.stack(b_s), jnp.stack(c_p[0]), jnp.stack(c_s[0]), jnp.stack(c_p[1]),
            jnp.stack(c_s[1]), jnp.stack(c_p[2]), jnp.stack(c_s[2]))
```

```python
import functools
import math

import jax
import jax.numpy as jnp
from jax import lax
from jax.experimental import pallas as pl
from jax.experimental.pallas import tpu as pltpu

F32 = jnp.float32
BF16 = jnp.bfloat16
HIGHEST = lax.Precision.HIGHEST

HEAD_DIM = 128
GQA = 4
SWA_WINDOW = 128
MOBA_BLOCK = 256
MOBA_TOPK = 3
MOBA_Q_CHUNK = 8
DIL_PAIRS = ((128, 1), (512, 4), (2048, 16))
DIL_BAND = 128
REL_BUCKETS = 32
REL_MAX_DISTANCE = 2048
N_EXPERT_GROUPS = 4
LN_EPS = 1e-5
LAYER_KINDS = 3

VMEM_LIMIT = 56 * 1024 * 1024


def _params(sem, vmem=VMEM_LIMIT):
    return pltpu.CompilerParams(dimension_semantics=sem, vmem_limit_bytes=vmem)


def _adaln_body(c_ref, w_ref, b_ref, o_ref):
    c = c_ref[...]
    a = c * jax.nn.sigmoid(c)
    o_ref[...] = jnp.dot(a.astype(BF16), w_ref[...].astype(BF16), preferred_element_type=F32) + b_ref[...]


def adaln_all(c, ada_w, ada_b, tn=512):
    n_layers, d, n = ada_w.shape
    tn = min(tn, n)
    r = c.shape[0]
    return pl.pallas_call(
        _adaln_body,
        out_shape=jax.ShapeDtypeStruct((n_layers, r, n), F32),
        grid=(n_layers, n // tn),
        in_specs=[
            pl.BlockSpec((r, d), lambda l, j: (0, 0)),
            pl.BlockSpec((None, d, tn), lambda l, j: (l, 0, j)),
            pl.BlockSpec((None, 1, tn), lambda l, j: (l, 0, j)),
        ],
        out_specs=pl.BlockSpec((None, r, tn), lambda l, j: (l, 0, j)),
        compiler_params=_params(("arbitrary", "arbitrary")),
        name="adaln",
    )(c, ada_w, ada_b.reshape(n_layers, 1, n))


def _proj_body(x_ref, w_ref, o_ref, wbf_ref, *, scale):
    @pl.when(pl.program_id(1) == 0)
    def _():
        wbf_ref[...] = w_ref[...].astype(BF16)

    acc = jnp.dot(x_ref[...], wbf_ref[...], preferred_element_type=F32)
    if scale != 1.0:
        acc = acc * scale
    o_ref[...] = acc.astype(o_ref.dtype)


def project(x, w, lead, col0, ncols, out_dtype, scale=1.0, tn=512):
    m, k = x.shape
    tm = min(m, 512)
    tn = min(tn, ncols)
    assert m % tm == 0 and ncols % tn == 0 and col0 % tn == 0
    j0 = col0 // tn
    return pl.pallas_call(
        functools.partial(_proj_body, scale=scale),
        out_shape=jax.ShapeDtypeStruct((m, ncols), out_dtype),
        grid=(ncols // tn, m // tm),
        in_specs=[
            pl.BlockSpec((tm, k), lambda j, i: (i, 0)),
            pl.BlockSpec((None, k, tn), lambda j, i: (lead, 0, j + j0)),
        ],
        out_specs=pl.BlockSpec((tm, tn), lambda j, i: (i, j)),
        scratch_shapes=[pltpu.VMEM((k, tn), BF16)],
        compiler_params=_params(("arbitrary", "arbitrary")),
        name="project",
    )(x, w)


def _layer_norm(v, g, b):
    mu = jnp.mean(v, axis=-1, keepdims=True)
    d = v - mu
    var = jnp.mean(d * d, axis=-1, keepdims=True)
    return d * lax.rsqrt(var + LN_EPS) * g + b


def _modulate_body(x_ref, shift_ref, scale_ref, h_ref):
    h_ref[...] = (x_ref[...] * (1.0 + scale_ref[...]) + shift_ref[...]).astype(h_ref.dtype)


def _mod_specs(mods, parts, tm, rows_per_batch, d):
    if mods.ndim == 3:
        tiles_per_batch = rows_per_batch // tm
        return [pl.BlockSpec((None, 1, d), functools.partial(
            lambda i, *_, p: (i // tiles_per_batch, 0, p), p=p)) for p in parts]
    return [pl.BlockSpec((tm, d), functools.partial(lambda i, *_, p: (i, p), p=p)) for p in parts]


def modulate(x, mods, rows_per_batch, tm=128):
    m, d = x.shape
    tm = min(tm, m)
    row = pl.BlockSpec((tm, d), lambda i: (i, 0))
    return pl.pallas_call(
        _modulate_body,
        out_shape=jax.ShapeDtypeStruct((m, d), BF16),
        grid=(m // tm,),
        in_specs=[row] + _mod_specs(mods, (0, 1), tm, rows_per_batch, d),
        out_specs=row,
        compiler_params=_params(("arbitrary",)),
        name="modulate",
    )(x, mods, mods)


def _post_attn_body(x_ref, y_ref, gate_ref, shift_ref, scale_ref, g_ref, b_ref, rw_ref,
                    xo_ref, h_ref, lg_ref, *, alpha):
    xm = _layer_norm(alpha * x_ref[...] + gate_ref[...] * y_ref[...], g_ref[...], b_ref[...])
    xo_ref[...] = xm
    h = xm * (1.0 + scale_ref[...]) + shift_ref[...]
    h_ref[...] = h
    lg_ref[...] = jnp.dot(h.astype(BF16), rw_ref[...].astype(BF16), preferred_element_type=F32)


def post_attention(x, y, mods, rows_per_batch, ln_g, ln_b, router_w, alpha, tm=128):
    m, d = x.shape
    tm = min(tm, m)
    n_exp = router_w.shape[1]
    row = pl.BlockSpec((tm, d), lambda i: (i, 0))
    vec = pl.BlockSpec((1, d), lambda i: (0, 0))
    return pl.pallas_call(
        functools.partial(_post_attn_body, alpha=alpha),
        out_shape=(jax.ShapeDtypeStruct((m, d), F32), jax.ShapeDtypeStruct((m, d), F32),
                   jax.ShapeDtypeStruct((m, n_exp), F32)),
        grid=(m // tm,),
        in_specs=[row, row] + _mod_specs(mods, (2, 3, 4), tm, rows_per_batch, d) + [
            vec, vec, pl.BlockSpec((d, n_exp), lambda i: (0, 0))],
        out_specs=(row, row, pl.BlockSpec((tm, n_exp), lambda i: (i, 0))),
        compiler_params=_params(("arbitrary",)),
        name="post_attention",
    )(x, y, mods, mods, mods, ln_g.reshape(1, d), ln_b.reshape(1, d), router_w)


def _post_moe_body(s0_ref, s1_ref, x_ref, gate_ref, g_ref, b_ref, *rest, alpha, tm, tok0, has_next):
    if has_next:
        shift_ref, scale_ref, slots_hbm, xo_ref, hn_ref, buf, sem = rest
    else:
        slots_hbm, xo_ref, buf, sem = rest
    i = pl.program_id(0)

    def copies(t, slot, r):
        tok = tok0 + t * tm + r
        return (pltpu.make_async_copy(slots_hbm.at[pl.ds(s0_ref[tok], 1)], buf.at[slot, 0, pl.ds(r, 1)], sem.at[slot]),
                pltpu.make_async_copy(slots_hbm.at[pl.ds(s1_ref[tok], 1)], buf.at[slot, 1, pl.ds(r, 1)], sem.at[slot]))

    def issue_tile(t, slot):
        def issue(r, c):
            a, b = copies(t, slot, r)
            a.start()
            b.start()
            return c

        lax.fori_loop(0, tm, issue, 0)

    @pl.when(i == 0)
    def _():
        issue_tile(0, 0)

    @pl.when(i + 1 < pl.num_programs(0))
    def _():
        issue_tile(i + 1, (i + 1) % 2)

    slot = i % 2

    def wait(r, c):
        a, b = copies(i, slot, r)
        a.wait()
        b.wait()
        return c

    lax.fori_loop(0, tm, wait, 0)
    f = buf[slot, 0] + buf[slot, 1]
    xn = _layer_norm(alpha * x_ref[...] + gate_ref[...] * f, g_ref[...], b_ref[...])
    xo_ref[...] = xn
    if has_next:
        hn_ref[...] = (xn * (1.0 + scale_ref[...]) + shift_ref[...]).astype(hn_ref.dtype)


def post_moe(x, slots, s0, s1, tok0, mods, mods_next, rows_per_batch, ln_g, ln_b, alpha, tm=128):
    m, d = x.shape
    tm = min(tm, m)
    has_next = mods_next is not None
    row = pl.BlockSpec((tm, d), lambda i, *_: (i, 0))
    vec = pl.BlockSpec((1, d), lambda i, *_: (0, 0))
    in_specs = [row] + _mod_specs(mods, (5,), tm, rows_per_batch, d) + [vec, vec]
    args = [x, mods, ln_g.reshape(1, d), ln_b.reshape(1, d)]
    out_shape = [jax.ShapeDtypeStruct((m, d), F32)]
    out_specs = [row]
    if has_next:
        in_specs += _mod_specs(mods_next, (0, 1), tm, rows_per_batch, d)
        args += [mods_next, mods_next]
        out_shape.append(jax.ShapeDtypeStruct((m, d), BF16))
        out_specs.append(row)
    in_specs.append(pl.BlockSpec(memory_space=pl.ANY))
    args.append(slots)
    out = pl.pallas_call(
        functools.partial(_post_moe_body, alpha=alpha, tm=tm, tok0=tok0, has_next=has_next),
        out_shape=tuple(out_shape),
        grid_spec=pltpu.PrefetchScalarGridSpec(
            num_scalar_prefetch=2, grid=(m // tm,), in_specs=in_specs, out_specs=tuple(out_specs),
            scratch_shapes=[pltpu.VMEM((2, 2, tm, d), F32), pltpu.SemaphoreType.DMA((2,))]),
        compiler_params=_params(("arbitrary",)),
        name="post_moe",
    )(s0, s1, *args)
    return out if has_next else (out[0], None)


def _route_body(lg_ref, bias_ref, ids_ref, w_ref, *, per_group):
    s = jax.nn.sigmoid(lg_ref[...])
    bz = s + bias_ref[...]
    n_exp = s.shape[0]
    rows_b = [bz[e:e + 1, :] for e in range(n_exp)]
    rows_s = [s[e:e + 1, :] for e in range(n_exp)]
    best = None
    for g in range(n_exp // per_group):
        members = range(g * per_group, (g + 1) * per_group)
        rank = {}
        for e in members:
            r = jnp.zeros_like(rows_b[e], dtype=jnp.int32)
            for o in members:
                if o == e:
                    continue
                ahead = (rows_b[o] > rows_b[e]) | ((rows_b[o] == rows_b[e]) & (o < e))
                r = r + ahead.astype(jnp.int32)
            rank[e] = r
        zero = jnp.zeros_like(rows_b[0])
        score = sum(jnp.where(rank[e] < 2, rows_b[e], zero) for e in members)
        e0 = sum(jnp.where(rank[e] == 0, e, 0) for e in members)
        e1 = sum(jnp.where(rank[e] == 1, e, 0) for e in members)
        s0 = sum(jnp.where(rank[e] == 0, rows_s[e], zero) for e in members)
        s1 = sum(jnp.where(rank[e] == 1, rows_s[e], zero) for e in members)
        if best is None:
            best = (score, e0, e1, s0, s1)
        else:
            take = score > best[0]
            best = tuple(jnp.where(take, new, old) for new, old in zip((score, e0, e1, s0, s1), best))
    _, e0, e1, s0, s1 = best
    tot = s0 + s1
    ids_ref[0:1, :] = e0
    ids_ref[1:2, :] = e1
    w_ref[0:1, :] = s0 / tot
    w_ref[1:2, :] = s1 / tot


def route(logits_t, router_bias):
    n_exp, n = logits_t.shape
    return pl.pallas_call(
        functools.partial(_route_body, per_group=n_exp // N_EXPERT_GROUPS),
        out_shape=(jax.ShapeDtypeStruct((2, n), jnp.int32), jax.ShapeDtypeStruct((2, n), F32)),
        name="route",
    )(logits_t, router_bias.reshape(n_exp, 1).astype(F32))


def _dispatch_body(tok_ref, nused_ref, src_a, src_b, o_ref, buf, sem, *, tm, n_a):
    i = pl.program_id(0)
    n_used = nused_ref[0]

    def row_copy(src, row, slot, r):
        return pltpu.make_async_copy(src.at[pl.ds(row, 1)], buf.at[slot, pl.ds(r, 1)], sem.at[slot])

    def issue_tile(t, slot):
        def issue(r, c):
            tok = tok_ref[t * tm + r]

            @pl.when(tok < n_a)
            def _():
                row_copy(src_a, tok, slot, r).start()

            @pl.when(tok >= n_a)
            def _():
                row_copy(src_b, tok - n_a, slot, r).start()

            return c

        lax.fori_loop(0, tm, issue, 0)

    @pl.when(i == 0)
    def _():
        issue_tile(0, 0)

    @pl.when(i + 1 < n_used)
    def _():
        issue_tile(i + 1, (i + 1) % 2)

    @pl.when(i < n_used)
    def _():
        slot = i % 2

        def wait(r, c):
            row_copy(src_a, 0, slot, r).wait()
            return c

        lax.fori_loop(0, tm, wait, 0)
        o_ref[...] = buf[slot].astype(o_ref.dtype)

    @pl.when(i >= n_used)
    def _():
        o_ref[...] = jnp.zeros_like(o_ref)


def dispatch(h_a, h_b, slot_tok, n_used, n_slots, tm):
    n_a, d = h_a.shape
    return pl.pallas_call(
        functools.partial(_dispatch_body, tm=tm, n_a=n_a),
        out_shape=jax.ShapeDtypeStruct((n_slots, d), BF16),
        grid_spec=pltpu.PrefetchScalarGridSpec(
            num_scalar_prefetch=2, grid=(n_slots // tm,),
            in_specs=[pl.BlockSpec(memory_space=pl.ANY), pl.BlockSpec(memory_space=pl.ANY)],
            out_specs=pl.BlockSpec((tm, d), lambda i, tok, nu: (i, 0)),
            scratch_shapes=[pltpu.VMEM((2, tm, d), F32), pltpu.SemaphoreType.DMA((2,))]),
        compiler_params=_params(("arbitrary",)),
        name="dispatch",
    )(slot_tok, n_used, h_a, h_b)


def _gate_up_body(te_ref, first_ref, nused_ref, x_ref, wg_ref, wu_ref, sw_ref, o_ref, wg_bf, wu_bf):
    t = pl.program_id(1)

    @pl.when(first_ref[t] == 1)
    def _():
        wg_bf[...] = wg_ref[...].astype(BF16)
        wu_bf[...] = wu_ref[...].astype(BF16)

    @pl.when(t < nused_ref[0])
    def _():
        x = x_ref[...]
        hg = jnp.dot(x, wg_bf[...], preferred_element_type=F32)
        hu = jnp.dot(x, wu_bf[...], preferred_element_type=F32)
        act = hg * jax.nn.sigmoid(hg) * hu * sw_ref[...]
        o_ref[...] = act.astype(o_ref.dtype)

    @pl.when(t >= nused_ref[0])
    def _():
        o_ref[...] = jnp.zeros_like(o_ref)


def expert_gate_up(xs, w_gate, w_up, layer, slot_w, tile_e, tile_first, n_used, tm, tf=512):
    n_slots, d = xs.shape
    f = w_gate.shape[3]
    tf = min(tf, f)
    n_tiles = n_slots // tm

    def tile(t, nu):
        return jnp.minimum(t, nu[0] - 1)

    w_spec = pl.BlockSpec((None, None, d, tf), lambda j, t, te, fi, nu: (layer, te[t], 0, j))
    return pl.pallas_call(
        _gate_up_body,
        out_shape=jax.ShapeDtypeStruct((n_slots, f), BF16),
        grid_spec=pltpu.PrefetchScalarGridSpec(
            num_scalar_prefetch=3, grid=(f // tf, n_tiles),
            in_specs=[pl.BlockSpec((tm, d), lambda j, t, te, fi, nu: (tile(t, nu), 0)), w_spec, w_spec,
                      pl.BlockSpec((tm, 1), lambda j, t, te, fi, nu: (tile(t, nu), 0))],
            out_specs=pl.BlockSpec((tm, tf), lambda j, t, te, fi, nu: (t, j)),
            scratch_shapes=[pltpu.VMEM((d, tf), BF16), pltpu.VMEM((d, tf), BF16)]),
        compiler_params=_params(("arbitrary", "arbitrary")),
        name="expert_gate_up",
    )(tile_e, tile_first, n_used, xs, w_gate, w_up, slot_w)


def _down_body(te_ref, first_ref, nused_ref, a_ref, w_ref, o_ref, w_bf):
    t = pl.program_id(1)

    @pl.when(first_ref[t] == 1)
    def _():
        w_bf[...] = w_ref[...].astype(BF16)

    @pl.when(t < nused_ref[0])
    def _():
        o_ref[...] = jnp.dot(a_ref[...], w_bf[...], preferred_element_type=F32)

    @pl.when(t >= nused_ref[0])
    def _():
        o_ref[...] = jnp.zeros_like(o_ref)


def expert_down(act, w_down, layer, tile_e, tile_first, n_used, tm, tn=2048):
    n_slots, f = act.shape
    d = w_down.shape[3]
    tn = min(tn, d)
    n_tiles = n_slots // tm

    def tile(t, nu):
        return jnp.minimum(t, nu[0] - 1)

    return pl.pallas_call(
        _down_body,
        out_shape=jax.ShapeDtypeStruct((n_slots, d), F32),
        grid_spec=pltpu.PrefetchScalarGridSpec(
            num_scalar_prefetch=3, grid=(d // tn, n_tiles),
            in_specs=[pl.BlockSpec((tm, f), lambda j, t, te, fi, nu: (tile(t, nu), 0)),
                      pl.BlockSpec((None, None, f, tn), lambda j, t, te, fi, nu: (layer, te[t], 0, j))],
            out_specs=pl.BlockSpec((tm, tn), lambda j, t, te, fi, nu: (t, j)),
            scratch_shapes=[pltpu.VMEM((f, tn), BF16)]),
        compiler_params=_params(("arbitrary", "arbitrary")),
        name="expert_down",
    )(tile_e, tile_first, n_used, act, w_down)


def moe_plan(ids, wts, n_exp, tm):
    n = ids.shape[1]
    n_pairs = 2 * n
    n_tiles = -(-(n_pairs + n_exp * (tm - 1)) // tm)
    n_slots = n_tiles * tm
    e_flat = ids.reshape(-1)
    onehot = (e_flat[:, None] == jnp.arange(n_exp, dtype=jnp.int32)[None, :]).astype(jnp.int32)
    csum = jnp.cumsum(onehot, axis=0)
    rank = jnp.sum(onehot * csum, axis=1) - 1
    counts = csum[-1]
    padded = ((counts + tm - 1) // tm) * tm
    ends = jnp.cumsum(padded)
    starts = ends - padded
    slot = (jnp.sum(onehot * starts[None, :], axis=1) + rank).astype(jnp.int32)
    tok = jnp.tile(jnp.arange(n, dtype=jnp.int32), 2)
    per_slot = jnp.zeros((n_slots, 2), F32).at[slot].set(jnp.stack([tok.astype(F32), wts.reshape(-1)], axis=1))
    slot_tok = per_slot[:, 0].astype(jnp.int32)
    slot_w = per_slot[:, 1]
    n_used = (ends[-1] // tm).astype(jnp.int32)
    tile_start = jnp.arange(n_tiles, dtype=jnp.int32) * tm
    tile_start = jnp.minimum(tile_start, ends[-1] - 1)
    tile_e = jnp.minimum(jnp.sum((tile_start[:, None] >= ends[None, :]).astype(jnp.int32), axis=1), n_exp - 1)
    tile_first = jnp.concatenate([jnp.ones((1,), jnp.int32), (tile_e[1:] != tile_e[:-1]).astype(jnp.int32)])
    return dict(n_slots=n_slots, slot_tok=slot_tok, slot_w=slot_w.reshape(n_slots, 1), n_used=n_used.reshape(1),
                tile_e=tile_e.astype(jnp.int32), tile_first=tile_first, s0=slot[:n], s1=slot[n:])


def moe_experts(h_a, h_b, logits, router_bias, w_gate, w_up, w_down, layer, tm=256):
    n_exp = w_gate.shape[1]
    ids, wts = route(logits.T, router_bias)
    plan = moe_plan(ids, wts, n_exp, tm)
    xs = dispatch(h_a, h_b, plan["slot_tok"], plan["n_used"], plan["n_slots"], tm)
    act = expert_gate_up(xs, w_gate, w_up, layer, plan["slot_w"], plan["tile_e"], plan["tile_first"],
                         plan["n_used"], tm)
    out = expert_down(act, w_down, layer, plan["tile_e"], plan["tile_first"], plan["n_used"], tm)
    return out, plan["s0"], plan["s1"]


NEG_INF = float("-inf")
_NT = (((1,), (1,)), ((), ()))


def _heads_first(vals, n_kv):
    lead = vals.ndim - 1
    v = jnp.transpose(vals, (lead,) + tuple(range(lead)))
    return v.reshape((n_kv, v.shape[0] // n_kv) + v.shape[1:])


def _toeplitz(vecs, n):
    lead = vecs.shape[:-1]
    u = jnp.pad(jnp.flip(vecs, -1), [(0, 0)] * len(lead) + [(0, 1)])
    rows = jnp.broadcast_to(u[..., None, :], lead + (n, 2 * n)).reshape(lead + (2 * n * n,))
    skew = rows[..., :n * (2 * n - 1)].reshape(lead + (n, 2 * n - 1))
    return skew[..., n - 1:]


def band_bias(rel_table, n_kv, max_step, dil):
    w = DIL_BAND
    steps = jnp.arange(-(2 * w - 1), 2 * w)
    vals = rel_table.astype(F32)[rel_bucket(steps * dil)]
    vals = jnp.where(((steps >= 0) & (steps <= max_step))[:, None], vals, NEG_INF)
    v = _toeplitz(_heads_first(vals, n_kv), 2 * w)[:, :, w:, :]
    return v.reshape(n_kv, v.shape[1] * w, 2 * w)


def moba_bias(rel_table, n_kv, n_blocks):
    blk = MOBA_BLOCK
    dist = jnp.arange(n_blocks)[:, None] * blk + jnp.arange(-(blk - 1), blk)[None, :]
    vals = rel_table.astype(F32)[rel_bucket(dist)]
    vals = jnp.where((dist >= 0)[..., None], vals, NEG_INF)
    v = jnp.transpose(_heads_first(vals, n_kv), (0, 2, 1, 3))
    v = _toeplitz(v, blk)
    return v.reshape(n_kv, n_blocks, v.shape[2] * blk, blk)


BAND_UNROLL = 3


def _stack_heads(q_ref, r0, rows, g_heads):
    return jnp.concatenate([q_ref[pl.ds(r0, rows), g * HEAD_DIM:(g + 1) * HEAD_DIM] for g in range(g_heads)], axis=0)


def _band_body(*refs, n_blocks, g_heads, with_sink):
    if with_sink:
        q_ref, k_ref, v_ref, bias_ref, sink_ref, o_ref, kb, vb = refs
        lse_ref = None
    else:
        q_ref, k_ref, v_ref, bias_ref, o_ref, lse_ref, kb, vb = refs
    w = DIL_BAND
    kb[...] = k_ref[...].astype(BF16)
    vb[...] = v_ref[...].astype(BF16)

    def block(n, first):
        r0 = 0 if first else pl.multiple_of(n * w, w)
        qs = _stack_heads(q_ref, r0, w, g_heads)
        if first:
            keys, vals, bias = kb[0:w, :], vb[0:w, :], bias_ref[:, w:]
        else:
            keys, vals, bias = kb[pl.ds(r0 - w, 2 * w), :], vb[pl.ds(r0 - w, 2 * w), :], bias_ref[...]
        s = lax.dot_general(qs, keys, _NT, preferred_element_type=F32) + bias
        m = jnp.max(s, axis=1, keepdims=True)
        if with_sink:
            sink = sink_ref[...]
            m = jnp.maximum(m, sink)
            p = jnp.exp(s - m)
            den = jnp.sum(p, axis=1, keepdims=True) + jnp.exp(sink - m)
            p = p * (1.0 / den)
        else:
            lse = jnp.log(jnp.sum(jnp.exp(s - m), axis=1, keepdims=True)) + m
            p = jnp.exp(s - lse)
        out = jnp.dot(p.astype(BF16), vals, preferred_element_type=F32)
        for g in range(g_heads):
            cols = slice(g * HEAD_DIM, (g + 1) * HEAD_DIM)
            o_ref[pl.ds(r0, w), cols] = out[g * w:(g + 1) * w].astype(o_ref.dtype)
        if lse_ref is not None:
            lane = lax.broadcasted_iota(jnp.int32, (w, g_heads), 1)
            tile = jnp.zeros((w, g_heads), F32)
            for g in range(g_heads):
                tile = jnp.where(lane == g, lse[g * w:(g + 1) * w], tile)
            lse_ref[pl.ds(r0, w), :] = tile

    block(0, True)
    if n_blocks > 1:
        def body(n, c):
            block(n, False)
            return c
        lax.fori_loop(1, n_blocks, body, 0, unroll=BAND_UNROLL)


def band_attention(q2, kv2, bias, sinks, *, batch, dil, n_kv, k_col0, v_col0, kv_cols, out_dtype):
    n_tok, qd = q2.shape
    seq = n_tok // batch
    m = seq // dil
    g_heads = qd // HEAD_DIM // n_kv
    gw = g_heads * HEAD_DIM
    assert m % DIL_BAND == 0
    qv = q2.reshape(batch * m, dil * qd)
    kvv = kv2.reshape(batch * m, dil * kv_cols)
    q_per = qd // gw
    kv_per = kv_cols // HEAD_DIM
    kc, vc = k_col0 // HEAD_DIM, v_col0 // HEAD_DIM
    with_sink = sinks is not None
    q_spec = pl.BlockSpec((m, gw), lambda k, r, b: (b, r * q_per + k))
    in_specs = [q_spec,
                pl.BlockSpec((m, HEAD_DIM), lambda k, r, b: (b, r * kv_per + kc + k)),
                pl.BlockSpec((m, HEAD_DIM), lambda k, r, b: (b, r * kv_per + vc + k)),
                pl.BlockSpec((None, gw, 2 * DIL_BAND), lambda k, r, b: (k, 0, 0))]
    args = [qv, kvv, kvv, bias]
    out_shape = [jax.ShapeDtypeStruct(qv.shape, out_dtype)]
    out_specs = [q_spec]
    if with_sink:
        in_specs.append(pl.BlockSpec((None, gw, 1), lambda k, r, b: (k, 0, 0)))
        args.append(jnp.repeat(sinks.astype(F32), DIL_BAND).reshape(n_kv, gw, 1))
    else:
        out_shape.append(jax.ShapeDtypeStruct((n_kv, dil, batch * m, g_heads), F32))
        out_specs.append(pl.BlockSpec((None, None, m, g_heads), lambda k, r, b: (k, r, b, 0)))
    out = pl.pallas_call(
        functools.partial(_band_body, n_blocks=m // DIL_BAND, g_heads=g_heads, with_sink=with_sink),
        out_shape=tuple(out_shape),
        grid=(n_kv, dil, batch),
        in_specs=in_specs,
        out_specs=tuple(out_specs),
        scratch_shapes=[pltpu.VMEM((m, HEAD_DIM), BF16), pltpu.VMEM((m, HEAD_DIM), BF16)],
        compiler_params=_params(("arbitrary", "arbitrary", "arbitrary")),
        name="band_attention",
    )(*args)
    if with_sink:
        return (out[0].reshape(n_tok, qd),)
    lse = jnp.transpose(out[1].reshape(n_kv, dil, batch, m, g_heads), (2, 3, 1, 0, 4)).reshape(n_tok, qd // HEAD_DIM)
    return out[0].reshape(n_tok, qd), lse


def _merge_body(o0, o1, o2, l0, l1, l2, out_ref):
    a, b, c = l0[...], l1[...], l2[...]
    mx = jnp.maximum(jnp.maximum(a, b), c)
    ea, eb, ec = jnp.exp(a - mx), jnp.exp(b - mx), jnp.exp(c - mx)
    inv = 1.0 / (ea + eb + ec)
    wa, wb, wc = ea * inv, eb * inv, ec * inv
    for h in range(a.shape[1]):
        cols = slice(h * HEAD_DIM, (h + 1) * HEAD_DIM)
        out_ref[:, cols] = (wa[:, h:h + 1] * o0[:, cols] + wb[:, h:h + 1] * o1[:, cols]
                            + wc[:, h:h + 1] * o2[:, cols]).astype(out_ref.dtype)


def merge_groups(outs, lses, tm=128):
    m, d = outs[0].shape
    n_heads = lses[0].shape[1]
    tm = min(tm, m)
    row = pl.BlockSpec((tm, d), lambda i: (i, 0))
    lrow = pl.BlockSpec((tm, n_heads), lambda i: (i, 0))
    return pl.pallas_call(
        _merge_body,
        out_shape=jax.ShapeDtypeStruct((m, d), BF16),
        grid=(m // tm,),
        in_specs=[row] * 3 + [lrow] * 3,
        out_specs=row,
        compiler_params=_params(("arbitrary",)),
        name="merge_groups",
    )(*outs, *lses)


def _moba_body(q_ref, k_ref, v_ref, bias_ref, o_ref, kb, vb, ls, *, n_blocks, g_heads):
    blk = MOBA_BLOCK
    rows = g_heads * blk
    kf = k_ref[...]
    kb[...] = kf.astype(BF16)
    vb[...] = v_ref[...].astype(BF16)
    kmean = jnp.concatenate(
        [jnp.sum(kf[n * blk:(n + 1) * blk], axis=0, keepdims=True) * (1.0 / blk) for n in range(n_blocks)],
        axis=0).astype(BF16)
    lane = lax.broadcasted_iota(jnp.int32, (blk, n_blocks), 1)
    lane_s = lax.broadcasted_iota(jnp.int32, (rows, n_blocks), 1)

    def own_block(ob, carry):
        r0 = pl.multiple_of(ob * blk, blk)
        qs = _stack_heads(q_ref, r0, blk, g_heads)
        gate_h = lax.dot_general(qs, kmean, _NT, preferred_element_type=F32)
        gate = gate_h[0:blk]
        for g in range(1, g_heads):
            gate = gate + gate_h[g * blk:(g + 1) * blk]
        gate = jnp.where(lane < ob, gate, NEG_INF)
        rank = jnp.zeros((blk, n_blocks), jnp.int32)
        for m_ in range(n_blocks):
            col = gate[:, m_:m_ + 1]
            rank = rank + ((col > gate) | ((col == gate) & (lane > m_))).astype(jnp.int32)
        sel = ((lane < ob) & (rank < MOBA_TOPK)).astype(F32)
        sel = jnp.concatenate([sel] * g_heads, axis=0)

        def logits(n, mrow):
            keys = kb[pl.ds(pl.multiple_of(n * blk, blk), blk), :]
            s = lax.dot_general(qs, keys, _NT, preferred_element_type=F32) + bias_ref[ob - n]
            chosen = jnp.sum(jnp.where(lane_s == n, sel, 0.0), axis=1, keepdims=True)
            s = jnp.where((chosen > 0.0) | (n == ob), s, NEG_INF)
            ls[n] = s
            return jnp.maximum(mrow, jnp.maximum(s[:, :HEAD_DIM], s[:, HEAD_DIM:]))

        mrow = lax.fori_loop(0, ob + 1, logits, jnp.full((rows, HEAD_DIM), NEG_INF, F32))
        m = jnp.max(mrow, axis=1, keepdims=True)

        def expo(n, lrow):
            e = jnp.exp(ls[n] - m)
            ls[n] = e
            return lrow + (e[:, :HEAD_DIM] + e[:, HEAD_DIM:])

        lrow = lax.fori_loop(0, ob + 1, expo, jnp.zeros((rows, HEAD_DIM), F32))
        inv = 1.0 / jnp.sum(lrow, axis=1, keepdims=True)

        def weigh(n, acc):
            p = (ls[n] * inv).astype(BF16)
            vals = vb[pl.ds(pl.multiple_of(n * blk, blk), blk), :]
            return acc + jnp.dot(p, vals, preferred_element_type=F32)

        acc = lax.fori_loop(0, ob + 1, weigh, jnp.zeros((rows, HEAD_DIM), F32))
        for g in range(g_heads):
            o_ref[pl.ds(r0, blk), g * HEAD_DIM:(g + 1) * HEAD_DIM] = acc[g * blk:(g + 1) * blk].astype(o_ref.dtype)
        return carry

    lax.fori_loop(0, n_blocks, own_block, 0)


def moba_prompt(q2, kv2, bias, *, batch, n_kv):
    n_tok, qd = q2.shape
    seq = n_tok // batch
    n_blocks = seq // MOBA_BLOCK
    g_heads = qd // HEAD_DIM // n_kv
    gw = g_heads * HEAD_DIM
    q_spec = pl.BlockSpec((seq, gw), lambda k, b: (b, k))
    return pl.pallas_call(
        functools.partial(_moba_body, n_blocks=n_blocks, g_heads=g_heads),
        out_shape=jax.ShapeDtypeStruct((n_tok, qd), BF16),
        grid=(n_kv, batch),
        in_specs=[q_spec,
                  pl.BlockSpec((seq, HEAD_DIM), lambda k, b: (b, k)),
                  pl.BlockSpec((seq, HEAD_DIM), lambda k, b: (b, n_kv + k)),
                  pl.BlockSpec((None, n_blocks, g_heads * MOBA_BLOCK, MOBA_BLOCK), lambda k, b: (k, 0, 0, 0))],
        out_specs=q_spec,
        scratch_shapes=[pltpu.VMEM((seq, HEAD_DIM), BF16), pltpu.VMEM((seq, HEAD_DIM), BF16),
                        pltpu.VMEM((n_blocks, g_heads * MOBA_BLOCK, MOBA_BLOCK), F32)],
        compiler_params=_params(("arbitrary", "arbitrary")),
        name="moba_prompt",
    )(q2, kv2, kv2, bias)


def _rows_of_kv_head(n_heads, n_kv, k):
    g = n_heads // n_kv
    row = lax.broadcasted_iota(jnp.int32, (n_heads, 1), 0)
    return (row >= k * g) & (row < (k + 1) * g)


def _per_kv_head_dot(lhs, rhs_of, n_kv, nt):
    n_heads = lhs.shape[0]
    out = None
    for k in range(n_kv):
        rhs = rhs_of(k)
        if nt:
            r = lax.dot_general(lhs, rhs, _NT, preferred_element_type=F32)
        else:
            r = jnp.dot(lhs, rhs, preferred_element_type=F32)
        out = r if out is None else jnp.where(_rows_of_kv_head(n_heads, n_kv, k), r, out)
    return out


def _swa_sample_body(q_ref, kv_ref, bias_ref, sink_ref, o_ref, *, n_kv):
    q = q_ref[...]
    s = _per_kv_head_dot(q, lambda k: kv_ref[:, k * HEAD_DIM:(k + 1) * HEAD_DIM].astype(BF16), n_kv, True)
    s = s + bias_ref[...]
    sink = sink_ref[...]
    m = jnp.maximum(jnp.max(s, axis=1, keepdims=True), sink)
    p = jnp.exp(s - m)
    den = jnp.sum(p, axis=1, keepdims=True) + jnp.exp(sink - m)
    p = (p * (1.0 / den)).astype(BF16)
    o = _per_kv_head_dot(p, lambda k: kv_ref[:, (n_kv + k) * HEAD_DIM:(n_kv + k + 1) * HEAD_DIM].astype(BF16),
                         n_kv, False)
    o_ref[...] = o.astype(o_ref.dtype)


def swa_sample_attention(q3, window_kv, rel_table, sinks, n_kv):
    bsz, n_heads, _ = q3.shape
    w = window_kv.shape[1]
    bias = rel_table.astype(F32)[rel_bucket(w - 1 - jnp.arange(w))].T
    return pl.pallas_call(
        functools.partial(_swa_sample_body, n_kv=n_kv),
        out_shape=jax.ShapeDtypeStruct(q3.shape, BF16),
        grid=(bsz,),
        in_specs=[pl.BlockSpec((None, n_heads, HEAD_DIM), lambda b: (b, 0, 0)),
                  pl.BlockSpec((None, w, window_kv.shape[2]), lambda b: (b, 0, 0)),
                  pl.BlockSpec((n_heads, w), lambda b: (0, 0)),
                  pl.BlockSpec((n_heads, 1), lambda b: (0, 0))],
        out_specs=pl.BlockSpec((None, n_heads, HEAD_DIM), lambda b: (b, 0, 0)),
        compiler_params=_params(("arbitrary",)),
        name="swa_sample",
    )(q3, window_kv, bias, sinks.astype(F32).reshape(n_heads, 1))


def _with_new_row(old, new_row):
    pad = jnp.where(lax.broadcasted_iota(jnp.int32, (8, HEAD_DIM), 0) == 0,
                    jnp.broadcast_to(new_row, (8, HEAD_DIM)), 0.0)
    return jnp.concatenate([old, pad], axis=0).astype(BF16)


def _dil_sample_body(q_ref, b0_ref, b1_ref, b2_ref, new_ref, bias_ref, o_ref, *, n_kv):
    q = q_ref[...]
    gcols = 2 * n_kv * HEAD_DIM
    outs, lses = [], []
    for gi, buf in enumerate((b0_ref, b1_ref, b2_ref)):
        def keys(k, buf=buf, gi=gi):
            c = gi * gcols + k * HEAD_DIM
            return _with_new_row(buf[:, 0, k, :], new_ref[:, c:c + HEAD_DIM])

        def vals(k, buf=buf, gi=gi):
            c = gi * gcols + (n_kv + k) * HEAD_DIM
            return _with_new_row(buf[:, 1, k, :], new_ref[:, c:c + HEAD_DIM])

        s = _per_kv_head_dot(q, keys, n_kv, True) + bias_ref[gi]
        m = jnp.max(s, axis=1, keepdims=True)
        lse = jnp.log(jnp.sum(jnp.exp(s - m), axis=1, keepdims=True)) + m
        p = jnp.exp(s - lse).astype(BF16)
        outs.append(_per_kv_head_dot(p, vals, n_kv, False))
        lses.append(lse)
    mx = jnp.maximum(jnp.maximum(lses[0], lses[1]), lses[2])
    e = [jnp.exp(l - mx) for l in lses]
    inv = 1.0 / (e[0] + e[1] + e[2])
    o_ref[...] = ((e[0] * inv) * outs[0] + (e[1] * inv) * outs[1] + (e[2] * inv) * outs[2]).astype(o_ref.dtype)


def dilated_sample_attention(q3, bufs, new_kv, rel_table, n_kv):
    bsz, n_heads, _ = q3.shape
    gcols = 2 * n_kv * HEAD_DIM
    steps = jnp.arange(DIL_BAND, 0, -1)
    bias = []
    for _, dil in DIL_PAIRS:
        old = rel_table.astype(F32)[rel_bucket(dil * steps)].T
        new = rel_table.astype(F32)[rel_bucket(jnp.zeros((1,), jnp.int32))].T
        bias.append(jnp.concatenate([old, new, jnp.full((n_heads, 7), NEG_INF, F32)], axis=1))
    bias = jnp.stack(bias)
    views = []
    for (window, dil), buf in zip(DIL_PAIRS, bufs):
        assert buf.shape[1] == window == DIL_BAND * dil
        views.append(buf.reshape((bsz, DIL_BAND, dil) + buf.shape[2:]))
    buf_spec = pl.BlockSpec((None, DIL_BAND, None, 2, n_kv, HEAD_DIM), lambda b: (b, 0, 0, 0, 0, 0))
    return pl.pallas_call(
        functools.partial(_dil_sample_body, n_kv=n_kv),
        out_shape=jax.ShapeDtypeStruct(q3.shape, BF16),
        grid=(bsz,),
        in_specs=[pl.BlockSpec((None, n_heads, HEAD_DIM), lambda b: (b, 0, 0)), buf_spec, buf_spec, buf_spec,
                  pl.BlockSpec((None, 1, new_kv.shape[1]), lambda b: (b, 0, 0)),
                  pl.BlockSpec(bias.shape, lambda b: (0, 0, 0))],
        out_specs=pl.BlockSpec((None, n_heads, HEAD_DIM), lambda b: (b, 0, 0)),
        compiler_params=_params(("arbitrary",)),
        name="dilated_sample",
    )(q3, *views, new_kv.reshape(bsz, 1, -1), bias)


PAGES_PER_STEP = 8


def _kmean_body(pt_ref, *refs, pages_per_block):
    page_refs, o_ref = refs[:-1], refs[-1]
    j = pl.program_id(1)
    blocks_per_step = len(page_refs) // pages_per_block
    for i in range(blocks_per_step):
        tot = None
        for p in page_refs[i * pages_per_block:(i + 1) * pages_per_block]:
            part = jnp.sum(p[...], axis=0)
            tot = part if tot is None else tot + part
        o_ref[j * blocks_per_step + i] = tot * (1.0 / MOBA_BLOCK)


def paged_block_means(pages, slot, page_table):
    page, n_kv = pages.shape[2], pages.shape[4]
    bsz, n_pages = page_table.shape
    pages_per_block = MOBA_BLOCK // page
    assert n_pages % PAGES_PER_STEP == 0 and PAGES_PER_STEP % pages_per_block == 0
    n_blocks = n_pages // pages_per_block
    specs = [pl.BlockSpec((None, None, page, None, n_kv, HEAD_DIM), functools.partial(
        lambda b, j, pt, i: (slot, pt[b * n_pages + j * PAGES_PER_STEP + i], 0, 0, 0, 0), i=i))
        for i in range(PAGES_PER_STEP)]
    return pl.pallas_call(
        functools.partial(_kmean_body, pages_per_block=pages_per_block),
        out_shape=jax.ShapeDtypeStruct((bsz, n_blocks, n_kv, HEAD_DIM), F32),
        grid_spec=pltpu.PrefetchScalarGridSpec(
            num_scalar_prefetch=1, grid=(bsz, n_pages // PAGES_PER_STEP), in_specs=specs,
            out_specs=pl.BlockSpec((None, n_blocks, n_kv, HEAD_DIM), lambda b, j, pt: (b, 0, 0, 0))),
        compiler_params=_params(("arbitrary", "arbitrary")),
        name="paged_block_means",
    )(page_table.reshape(-1), *([pages] * PAGES_PER_STEP))


def _moba_select_body(q_ref, km_ref, sel_ref, *, n_kv):
    q = q_ref[...]
    n_heads = q.shape[0]
    n_blocks = km_ref.shape[1]
    rows = []
    for k in range(n_kv):
        km = km_ref[k].astype(BF16)
        g_all = lax.dot_general(q, km, _NT, preferred_element_type=F32)
        rows.append(jnp.sum(jnp.where(_rows_of_kv_head(n_heads, n_kv, k), g_all, 0.0), axis=0, keepdims=True))
    gate = jnp.concatenate(rows, axis=0)
    lane = lax.broadcasted_iota(jnp.int32, gate.shape, 1)
    rank = jnp.zeros(gate.shape, jnp.int32)
    for m_ in range(n_blocks):
        col = gate[:, m_:m_ + 1]
        rank = rank + ((col > gate) | ((col == gate) & (lane > m_))).astype(jnp.int32)
    out_lane = lax.broadcasted_iota(jnp.int32, sel_ref.shape, 1)
    out = jnp.zeros(sel_ref.shape, jnp.int32)
    for r in range(MOBA_TOPK):
        idx = jnp.sum(jnp.where(rank == r, lane, 0), axis=1, keepdims=True)
        out = jnp.where(out_lane == r, idx, out)
    sel_ref[...] = out


def moba_select(q3, kmean):
    bsz, n_heads, _ = q3.shape
    n_kv, n_blocks = kmean.shape[1], kmean.shape[2]
    assert n_blocks >= MOBA_TOPK
    sel = pl.pallas_call(
        functools.partial(_moba_select_body, n_kv=n_kv),
        out_shape=jax.ShapeDtypeStruct((bsz, n_kv, HEAD_DIM), jnp.int32),
        grid=(bsz,),
        in_specs=[pl.BlockSpec((None, n_heads, HEAD_DIM), lambda b: (b, 0, 0)),
                  pl.BlockSpec((None, n_kv, n_blocks, HEAD_DIM), lambda b: (b, 0, 0, 0))],
        out_specs=pl.BlockSpec((None, n_kv, HEAD_DIM), lambda b: (b, 0, 0)),
        compiler_params=_params(("arbitrary",)),
        name="moba_select",
    )(q3, kmean)
    return sel[:, :, :MOBA_TOPK]


def _moba_sample_body(pt_ref, sel_ref, q_ref, pb_ref, new_ref, b0_ref, pages_hbm, o_ref, kbuf, vbuf, sem,
                      *, slot, n_kv, n_pages, page):
    b = pl.program_id(0)

    def block_of(k, j):
        return sel_ref[(b * n_kv + k) * MOBA_TOPK + j]

    def copies(k, j):
        out = []
        for half in range(MOBA_BLOCK // page):
            pg = pt_ref[b * n_pages + (MOBA_BLOCK // page) * block_of(k, j) + half]
            rows = pl.ds(half * page, page)
            out.append(pltpu.make_async_copy(pages_hbm.at[slot, pg, pl.ds(0, page), 0, k, :],
                                             kbuf.at[k, j, rows, :], sem.at[0]))
            out.append(pltpu.make_async_copy(pages_hbm.at[slot, pg, pl.ds(0, page), 1, k, :],
                                             vbuf.at[k, j, rows, :], sem.at[1]))
        return out

    for k in range(n_kv):
        for j in range(MOBA_TOPK):
            for c in copies(k, j):
                c.start()
    for k in range(n_kv):
        for j in range(MOBA_TOPK):
            for c in copies(k, j):
                c.wait()

    for k in range(n_kv):
        q = q_ref[k]
        kn = new_ref[k].astype(BF16).astype(F32)
        vn = new_ref[n_kv + k].astype(BF16).astype(F32)
        s_new = jnp.sum(q.astype(F32) * kn, axis=1, keepdims=True) + b0_ref[k]
        ss = [lax.dot_general(q, kbuf[k, j].astype(BF16), _NT, preferred_element_type=F32) + pb_ref[k, block_of(k, j)]
              for j in range(MOBA_TOPK)]
        m = s_new
        for s in ss:
            m = jnp.maximum(m, jnp.max(s, axis=1, keepdims=True))
        e_new = jnp.exp(s_new - m)
        es = [jnp.exp(s - m) for s in ss]
        tot = e_new
        for e in es:
            tot = tot + jnp.sum(e, axis=1, keepdims=True)
        inv = 1.0 / tot
        own_out = (e_new * inv).astype(BF16).astype(F32) * vn
        sel_out = None
        for j in range(MOBA_TOPK):
            part = jnp.dot((es[j] * inv).astype(BF16), vbuf[k, j].astype(BF16), preferred_element_type=F32)
            sel_out = part if sel_out is None else sel_out + part
        o_ref[k] = (sel_out + own_out).astype(o_ref.dtype)


def moba_sample_attention(q3, pages, slot, page_table, sel, new_kv, rel_table):
    bsz, n_heads, _ = q3.shape
    page, n_kv = pages.shape[2], pages.shape[4]
    g = n_heads // n_kv
    n_pages = page_table.shape[1]
    assert MOBA_BLOCK % page == 0
    past_len = n_pages * page
    n_blocks = past_len // MOBA_BLOCK
    pos = jnp.arange(past_len, dtype=jnp.int32)
    pb = rel_table.astype(F32)[rel_bucket(past_len - pos)]
    pb = jnp.transpose(pb.reshape(n_blocks, MOBA_BLOCK, n_kv, g), (2, 0, 3, 1))
    b0 = rel_table.astype(F32)[rel_bucket(jnp.zeros((), jnp.int32))].reshape(n_kv, g, 1)
    head = pl.BlockSpec((None, n_kv, g, HEAD_DIM), lambda b, pt, sl: (b, 0, 0, 0))
    out = pl.pallas_call(
        functools.partial(_moba_sample_body, slot=slot, n_kv=n_kv, n_pages=n_pages, page=page),
        out_shape=jax.ShapeDtypeStruct((bsz, n_kv, g, HEAD_DIM), BF16),
        grid_spec=pltpu.PrefetchScalarGridSpec(
            num_scalar_prefetch=2, grid=(bsz,),
            in_specs=[head,
                      pl.BlockSpec(pb.shape, lambda b, pt, sl: (0, 0, 0, 0)),
                      pl.BlockSpec((None, 2 * n_kv, 1, HEAD_DIM), lambda b, pt, sl: (b, 0, 0, 0)),
                      pl.BlockSpec(b0.shape, lambda b, pt, sl: (0, 0, 0)),
                      pl.BlockSpec(memory_space=pl.ANY)],
            out_specs=head,
            scratch_shapes=[pltpu.VMEM((n_kv, MOBA_TOPK, MOBA_BLOCK, HEAD_DIM), F32),
                            pltpu.VMEM((n_kv, MOBA_TOPK, MOBA_BLOCK, HEAD_DIM), F32),
                            pltpu.SemaphoreType.DMA((2,))]),
        compiler_params=_params(("arbitrary",)),
        name="moba_sample",
    )(page_table.reshape(-1), sel.reshape(-1), q3.reshape(bsz, n_kv, g, HEAD_DIM), pb,
      new_kv.reshape(bsz, 2 * n_kv, 1, HEAD_DIM), b0, pages)
    return out.reshape(bsz, n_heads, HEAD_DIM)


def rel_bucket(dist):
    n = jnp.maximum(dist, 0)
    exact = REL_BUCKETS // 2
    scaled = (jnp.log(jnp.maximum(n, exact).astype(F32) / exact)
              / math.log(REL_MAX_DISTANCE / exact) * (REL_BUCKETS - exact))
    large = jnp.minimum(exact + scaled.astype(jnp.int32), REL_BUCKETS - 1)
    return jnp.where(n < exact, n, large)


def rel_bias(dist, rel_table):
    return rel_table.astype(F32)[rel_bucket(dist)]


def sink_softmax(logits, sink):
    m = jnp.maximum(jnp.max(logits, axis=-1, keepdims=True), sink)
    p = jnp.exp(logits - m)
    return p / (jnp.sum(p, axis=-1, keepdims=True) + jnp.exp(sink - m))


def swa_prompt(q, kv, sinks, rel_table):
    b, s = q.shape[:2]
    n_heads = q.shape[2]
    kvh = kv.shape[3]
    gqa = n_heads // kvh
    w = SWA_WINDOW
    nb = s // w
    qb = q.reshape(b, nb, w, kvh, gqa, HEAD_DIM)
    kvb = kv.reshape(b, nb, w, 2, kvh, HEAD_DIM)
    prev = jnp.pad(kvb, ((0, 0), (1, 0), (0, 0), (0, 0), (0, 0), (0, 0)))[:, :-1]
    band = jnp.concatenate([prev, kvb], axis=2)
    logits = jnp.einsum('bnqkgd,bnskd->bnkgqs', qb, band[:, :, :, 0]).astype(F32)
    dist = (jnp.arange(w)[:, None] + w) - jnp.arange(2 * w)[None, :]
    key_pos = jnp.arange(nb)[:, None, None] * w - w + jnp.arange(2 * w)[None, None, :]
    keep = (dist >= 0) & (dist < w) & (key_pos >= 0)
    bias = jnp.transpose(rel_bias(dist, rel_table), (2, 0, 1)).reshape(kvh, gqa, w, 2 * w)
    logits = jnp.where(keep[None, :, None, None], logits + bias, -jnp.inf)
    p = sink_softmax(logits, sinks.astype(F32).reshape(kvh, gqa, 1, 1))
    out = jnp.einsum('bnkgqs,bnskd->bnqkgd', p.astype(kv.dtype), band[:, :, :, 1])
    return out.reshape(b, s, n_heads, HEAD_DIM)


def swa_sample(q, kv_new, buf, sinks, rel_table):
    b, t = q.shape[:2]
    n_heads = q.shape[2]
    kvh = buf.shape[3]
    gqa = n_heads // kvh
    lb = buf.shape[1]
    kk = jnp.concatenate([buf, kv_new], axis=1)
    dist = (lb + jnp.arange(t))[:, None] - jnp.arange(lb + t)[None, :]
    keep = (dist >= 0) & (dist < SWA_WINDOW)
    qg = q.reshape(b, t, kvh, gqa, HEAD_DIM)
    logits = jnp.einsum('btkgd,bskd->bkgts', qg, kk[:, :, 0]).astype(F32)
    bias = jnp.transpose(rel_bias(dist, rel_table), (2, 0, 1)).reshape(kvh, gqa, t, lb + t)
    logits = jnp.where(keep, logits + bias, -jnp.inf)
    p = sink_softmax(logits, sinks.astype(F32).reshape(kvh, gqa, 1, 1))
    out = jnp.einsum('bkgts,bskd->btkgd', p.astype(kk.dtype), kk[:, :, 1]).reshape(b, t, n_heads, HEAD_DIM)
    new_len = min(SWA_WINDOW, lb + t)
    return out, kk[:, lb + t - new_len:]


def moba_attention(q, kv, q_pos, rel_table, q_chunk):
    b, sq = q.shape[:2]
    n_heads = q.shape[2]
    kvh = kv.shape[3]
    gqa = n_heads // kvh
    L = kv.shape[1]
    nb = -(-L // MOBA_BLOCK)
    kvb = jnp.pad(kv, ((0, 0), (0, nb * MOBA_BLOCK - L), (0, 0), (0, 0), (0, 0)))
    kvb = kvb.reshape(b, nb, MOBA_BLOCK, 2, kvh, HEAD_DIM)
    kmean = jnp.mean(kvb[:, :, :, 0], axis=2, dtype=F32)
    n_sel = min(MOBA_TOPK, nb)
    tab = rel_table.astype(F32).T.reshape(kvh, gqa, REL_BUCKETS)
    b_i = jnp.arange(b)[:, None, None, None]
    kv_i = jnp.arange(kvh)[None, None, :, None]
    t_i = jnp.arange(MOBA_BLOCK)

    def chunk(args):
        qc, pc = args
        qc = qc.reshape(b, q_chunk, kvh, gqa, HEAD_DIM)
        own = pc[0] // MOBA_BLOCK
        gate = jnp.einsum('bqkgd,bnkd->bqkn', qc.astype(F32), kmean)
        gate = jnp.where(jnp.arange(nb) < own, gate, -jnp.inf)
        _, blk = lax.top_k(gate, n_sel)
        sel = kvb[b_i, blk, :, :, kv_i]
        s_log = jnp.einsum('bqkgd,bqkntd->bqkgnt', qc, sel[..., 0, :]).astype(F32)
        s_dist = pc[None, :, None, None, None] - (blk[..., None] * MOBA_BLOCK + t_i)
        s_bias = tab[jnp.arange(kvh)[:, None, None, None], jnp.arange(gqa)[:, None, None],
                     rel_bucket(s_dist)[:, :, :, None]]
        s_keep = (blk < own)[:, :, :, None, :, None]
        s_log = jnp.where(s_keep, s_log + s_bias, -jnp.inf)
        own_kv = lax.dynamic_index_in_dim(kvb, own, axis=1, keepdims=False)
        o_log = jnp.einsum('bqkgd,btkd->bqkgt', qc, own_kv[:, :, 0]).astype(F32)
        o_dist = pc[:, None] - (own * MOBA_BLOCK + t_i)[None, :]
        o_bias = jnp.moveaxis(rel_bias(o_dist, rel_table), -1, 1).reshape(q_chunk, kvh, gqa, MOBA_BLOCK)
        o_log = jnp.where(o_dist[:, None, None, :] >= 0, o_log + o_bias, -jnp.inf)
        logits = jnp.concatenate([s_log.reshape(b, q_chunk, kvh, gqa, n_sel * MOBA_BLOCK), o_log], axis=-1)
        p = jax.nn.softmax(logits, axis=-1).astype(kv.dtype)
        p_sel = p[..., :n_sel * MOBA_BLOCK].reshape(b, q_chunk, kvh, gqa, n_sel, MOBA_BLOCK)
        out = (jnp.einsum('bqkgnt,bqkntd->bqkgd', p_sel, sel[..., 1, :])
               + jnp.einsum('bqkgt,btkd->bqkgd', p[..., n_sel * MOBA_BLOCK:], own_kv[:, :, 1]))
        return out.reshape(b, q_chunk, n_heads, HEAD_DIM)

    nc = sq // q_chunk
    qs = jnp.moveaxis(q.reshape(b, nc, q_chunk, n_heads, HEAD_DIM), 1, 0)
    ps = q_pos.reshape(nc, q_chunk)
    out = lax.map(chunk, (qs, ps))
    return jnp.moveaxis(out, 0, 1).reshape(b, sq, n_heads, HEAD_DIM)


def dilated_prompt(q, kv, dil, rel_table):
    b, s = q.shape[:2]
    n_heads = q.shape[2]
    kvh = kv.shape[3]
    gqa = n_heads // kvh
    m = s // dil
    mb = -(-m // DIL_BAND)
    mp = mb * DIL_BAND

    def to_sub(x):
        x = jnp.moveaxis(x.reshape((b, m, dil) + x.shape[2:]), 2, 1)
        x = jnp.pad(x, [(0, 0), (0, 0), (0, mp - m)] + [(0, 0)] * (x.ndim - 3))
        return x.reshape((b, dil, mb, DIL_BAND) + x.shape[3:])

    qs = to_sub(q.reshape(b, s, kvh, gqa, HEAD_DIM))
    kvs = to_sub(kv)
    prev = jnp.pad(kvs, ((0, 0), (0, 0), (1, 0), (0, 0), (0, 0), (0, 0), (0, 0)))[:, :, :-1]
    band = jnp.concatenate([prev, kvs], axis=3)
    logits = jnp.einsum('brnqkgd,brnskd->brnkgqs', qs, band[:, :, :, :, 0]).astype(F32)
    steps = (jnp.arange(DIL_BAND)[:, None] + DIL_BAND) - jnp.arange(2 * DIL_BAND)[None, :]
    sub_pos = jnp.arange(mb)[:, None, None] * DIL_BAND - DIL_BAND + jnp.arange(2 * DIL_BAND)[None, None, :]
    keep = (steps >= 0) & (steps <= DIL_BAND) & (sub_pos >= 0)
    bias = jnp.transpose(rel_bias(steps * dil, rel_table), (2, 0, 1)).reshape(kvh, gqa, DIL_BAND, 2 * DIL_BAND)
    logits = jnp.where(keep[None, None, :, None, None], logits + bias, -jnp.inf)
    lse = jax.nn.logsumexp(logits, axis=-1)
    p = jnp.exp(logits - lse[..., None]).astype(kv.dtype)
    out = jnp.einsum('brnkgqs,brnskd->brnqkgd', p, band[:, :, :, :, 1])
    out = jnp.moveaxis(out.reshape(b, dil, mp, n_heads, HEAD_DIM)[:, :, :m], 1, 2).reshape(b, s, n_heads, HEAD_DIM)
    lse = jnp.moveaxis(lse, -1, 3).reshape(b, dil, mp, n_heads)[:, :, :m]
    lse = jnp.moveaxis(lse, 1, 2).reshape(b, s, n_heads)
    return out, lse


def dilated_sample(q, kv_new, buf, dil, window, rel_table):
    b, t = q.shape[:2]
    n_heads = q.shape[2]
    kvh = buf.shape[3]
    gqa = n_heads // kvh
    lb = buf.shape[1]
    kk = jnp.concatenate([buf, kv_new], axis=1)
    k_steps = jnp.arange(DIL_BAND + 1)
    idx = lb + jnp.arange(t)[:, None] - dil * k_steps[None, :]
    keep = idx >= 0
    g = kk[:, jnp.maximum(idx, 0)]
    qg = q.reshape(b, t, kvh, gqa, HEAD_DIM)
    logits = jnp.einsum('btkgd,btjkd->btkgj', qg, g[:, :, :, 0]).astype(F32)
    bias = rel_bias(dil * k_steps, rel_table).T.reshape(kvh, gqa, DIL_BAND + 1)
    logits = jnp.where(keep[None, :, None, None, :], logits + bias, -jnp.inf)
    lse = jax.nn.logsumexp(logits, axis=-1)
    p = jnp.exp(logits - lse[..., None]).astype(kk.dtype)
    out = jnp.einsum('btkgj,btjkd->btkgd', p, g[:, :, :, 1]).reshape(b, t, n_heads, HEAD_DIM)
    new_len = min(window, lb + t)
    return out, lse.reshape(b, t, n_heads), kk[:, lb + t - new_len:]


def merge_by_denominator(outs, lses):
    wts = jax.nn.softmax(jnp.stack(lses), axis=0)
    return jnp.sum(wts[..., None] * jnp.stack(outs).astype(F32), axis=0)


def kernel(x_prompt, x_sample, cache_a_kv, cache_b_kv_pages, cache_c_kv_g0, cache_c_kv_g1, cache_c_kv_g2,
           page_table, c_prompt, c_sample, rel_table, ada_w, ada_b, ln_g, ln_b,
           a_w_qkv, a_w_o, a_sinks, b_w_qkv, b_w_o, c_w_qkv, c_w_o,
           router_w, router_bias, moe_w_gate, moe_w_up, moe_w_down):
    bp, seq, d = x_prompt.shape
    bs, dseq, _ = x_sample.shape
    depth = ada_w.shape[0]
    n_heads = d // HEAD_DIM
    qd = n_heads * HEAD_DIM
    kvh = n_heads // GQA
    dil_kvh = kvh // 2
    n_dil = len(DIL_PAIRS)
    alpha = (2 * depth) ** 0.25
    q_scale = HEAD_DIM ** -0.5
    mp, ms = bp * seq, bs * dseq
    past_len = page_table.shape[1] * cache_b_kv_pages.shape[2]

    n_c = bp + bs
    c_rows = -(-n_c // 8) * 8
    c_all = jnp.concatenate([c_prompt, c_sample, jnp.zeros((c_rows - n_c, d), F32)], axis=0)
    mods_all = adaln_all(c_all, ada_w, ada_b)

    def layer_mods(i):
        return mods_all[i, :bp].reshape(bp, 1, 6 * d), mods_all[i, bp:n_c]

    xp = x_prompt.reshape(mp, d)
    xs = x_sample.reshape(ms, d)
    mods_p, mods_s = layer_mods(0)
    hp = modulate(xp, mods_p, seq)
    hs = modulate(xs, mods_s, 1)

    bias_a = band_bias(rel_table, kvh, SWA_WINDOW - 1, 1)
    bias_b = moba_bias(rel_table, kvh, seq // MOBA_BLOCK)
    bias_c = [band_bias(rel_table, dil_kvh, DIL_BAND, dil) for _, dil in DIL_PAIRS]

    a_p, a_s, b_p, b_s = [], [], [], []
    c_p = [[] for _ in range(n_dil)]
    c_s = [[] for _ in range(n_dil)]
    slot_of = [0] * LAYER_KINDS
    for i in range(depth):
        kind = i % LAYER_KINDS
        slot = slot_of[kind]
        slot_of[kind] += 1
        w_qkv, w_o = ((a_w_qkv, a_w_o), (b_w_qkv, b_w_o), (c_w_qkv, c_w_o))[kind]
        kv_cols = w_qkv.shape[2] - qd
        qp = project(hp, w_qkv, slot, 0, qd, BF16, q_scale)
        kvp2 = project(hp, w_qkv, slot, qd, kv_cols, F32)
        qs = project(hs, w_qkv, slot, 0, qd, BF16, q_scale).reshape(bs, n_heads, HEAD_DIM)
        kvs2 = project(hs, w_qkv, slot, qd, kv_cols, F32)
        if kind == 0:
            (op,) = band_attention(qp, kvp2, bias_a, a_sinks[slot], batch=bp, dil=1, n_kv=kvh, k_col0=0,
                                   v_col0=kvh * HEAD_DIM, kv_cols=kv_cols, out_dtype=BF16)
            kvp = kvp2.reshape(bp, seq, 2, kvh, HEAD_DIM)
            kvs = kvs2.reshape(bs, dseq, 2, kvh, HEAD_DIM)
            assert cache_a_kv.shape[2] == SWA_WINDOW
            new_buf = jnp.concatenate([cache_a_kv[slot][:, 1:], kvs], axis=1)
            os_ = swa_sample_attention(qs, new_buf.reshape(bs, SWA_WINDOW, kv_cols), rel_table, a_sinks[slot], kvh)
            a_p.append(kvp[:, seq - min(SWA_WINDOW, seq):])
            a_s.append(new_buf)
        elif kind == 1:
            op = moba_prompt(qp, kvp2, bias_b, batch=bp, n_kv=kvh)
            kmean = paged_block_means(cache_b_kv_pages, slot, page_table)
            sel = moba_select(qs, jnp.transpose(kmean, (0, 2, 1, 3)))
            os_ = moba_sample_attention(qs, cache_b_kv_pages, slot, page_table, sel, kvs2, rel_table)
            b_p.append(kvp2.reshape(bp, seq, 2, kvh, HEAD_DIM))
            b_s.append(kvs2.reshape(bs, dseq, 2, kvh, HEAD_DIM))
        else:
            kvp = kvp2.reshape(bp, seq, n_dil, 2, dil_kvh, HEAD_DIM)
            kvs = kvs2.reshape(bs, dseq, n_dil, 2, dil_kvh, HEAD_DIM)
            bufs = (cache_c_kv_g0[slot], cache_c_kv_g1[slot], cache_c_kv_g2[slot])
            op_l, lp_l = [], []
            group_cols = 2 * dil_kvh * HEAD_DIM
            for g, (window, dil) in enumerate(DIL_PAIRS):
                o, l = band_attention(qp, kvp2, bias_c[g], None, batch=bp, dil=dil, n_kv=dil_kvh,
                                      k_col0=g * group_cols, v_col0=g * group_cols + dil_kvh * HEAD_DIM,
                                      kv_cols=kv_cols, out_dtype=F32)
                op_l.append(o)
                lp_l.append(l)
                c_s[g].append(jnp.concatenate([bufs[g][:, 1:], kvs[:, :, g]], axis=1))
                c_p[g].append(kvp[:, seq - min(window, seq):, g])
            op = merge_groups(op_l, lp_l)
            os_ = dilated_sample_attention(qs, bufs, kvs2, rel_table, dil_kvh)
        yp = project(op, w_o, slot, 0, d, F32)
        ys = project(os_.reshape(ms, qd), w_o, slot, 0, d, F32)

        xp, h2p, lgp = post_attention(xp, yp, mods_p, seq, ln_g[i, 0], ln_b[i, 0], router_w, alpha)
        xs, h2s, lgs = post_attention(xs, ys, mods_s, 1, ln_g[i, 0], ln_b[i, 0], router_w, alpha)
        slots, s0, s1 = moe_experts(h2p, h2s, jnp.concatenate([lgp, lgs], axis=0),
                                    router_bias, moe_w_gate, moe_w_up, moe_w_down, i)
        if i + 1 < depth:
            nxt_p, nxt_s = layer_mods(i + 1)
        else:
            nxt_p = nxt_s = None
        xp, hp = post_moe(xp, slots, s0, s1, 0, mods_p, nxt_p, seq, ln_g[i, 1], ln_b[i, 1], alpha)
        xs, hs = post_moe(xs, slots, s0, s1, mp, mods_s, nxt_s, 1, ln_g[i, 1], ln_b[i, 1], alpha)
        mods_p, mods_s = nxt_p, nxt_s

    return (xp.reshape(bp, seq, d), xs.reshape(bs, dseq, d), jnp.stack(a_p), jnp.stack(a_s),
            jnp.stack(b_p), jnp.stack(b_s), jnp.stack(c_p[0]), jnp.stack(c_s[0]), jnp.stack(c_p[1]),
            jnp.stack(c_s[1]), jnp.stack(c_p[2]), jnp.stack(c_s[2]))
```
